```python
import math
import jax, jax.numpy as jnp
from jax import lax
import numpy as np

D_MODEL = 2048
BATCH = 2
SEQ = 8192
DEPTH = 4
DEC_BATCH = 32
DEC_SEQ = 64
PAST_LEN = 1024

CHUNK = 64
D_SSM = D_MODEL // 2
D_ATT = D_MODEL // 2
D_INNER = D_SSM + D_ATT
SSM_GROUP = 16
N_GROUPS = D_SSM // SSM_GROUP
STATE_DIM = 64
HEAD_DIM = 128
N_HEADS = D_ATT // HEAD_DIM
Q_BLOCK = 128
D_PROJ = 2 * D_SSM + 4 * D_ATT + N_HEADS
EPS = 1e-6
F_BIAS_INIT = 3.0
DT_MIN = 0.001
DT_MAX = 0.1

kernel_name = "hymba_s5_fox_streaming_encoder"


def _rmsnorm(x, g):
    xf = x.astype(jnp.float32)
    y = xf * lax.rsqrt(jnp.mean(xf * xf, axis=-1, keepdims=True) + EPS)
    return (y * g.astype(jnp.float32)).astype(x.dtype)


def _s5_scan(u, log_dt, a_re, a_im, b_re, b_im, c_re, c_im, d, h0_re, h0_im):
    f32 = jnp.float32
    bsz, L, _ = u.shape
    uf = u.astype(f32).reshape(bsz, L, N_GROUPS, SSM_GROUP)
    dt = jnp.exp(log_dt.astype(f32))[:, None]
    ar = a_re.astype(f32)
    ai = a_im.astype(f32)
    mag = jnp.exp(ar * dt)
    abar_re = mag * jnp.cos(ai * dt)
    abar_im = mag * jnp.sin(ai * dt)
    den = ar * ar + ai * ai
    nr = abar_re - 1.0
    ni = abar_im
    coef_re = (nr * ar + ni * ai) / den
    coef_im = (ni * ar - nr * ai) / den
    bu_re = jnp.einsum('blgc,gpc->blgp', uf, b_re.astype(f32))
    bu_im = jnp.einsum('blgc,gpc->blgp', uf, b_im.astype(f32))
    bb_re = coef_re * bu_re - coef_im * bu_im
    bb_im = coef_re * bu_im + coef_im * bu_re
    if h0_re is not None:
        h0r = h0_re.astype(f32)
        h0i = h0_im.astype(f32)
        bb_re = bb_re.at[:, 0].add(abar_re * h0r - abar_im * h0i)
        bb_im = bb_im.at[:, 0].add(abar_re * h0i + abar_im * h0r)
    a_re_t = jnp.broadcast_to(abar_re, bb_re.shape)
    a_im_t = jnp.broadcast_to(abar_im, bb_im.shape)

    def combine(e1, e2):
        a1r, a1i, b1r, b1i = e1
        a2r, a2i, b2r, b2i = e2
        return (a2r * a1r - a2i * a1i,
                a2r * a1i + a2i * a1r,
                a2r * b1r - a2i * b1i + b2r,
                a2r * b1i + a2i * b1r + b2i)

    _, _, h_re, h_im = lax.associative_scan(combine, (a_re_t, a_im_t, bb_re, bb_im), axis=1)
    y = (jnp.einsum('blgp,gcp->blgc', h_re, c_re.astype(f32))
         - jnp.einsum('blgp,gcp->blgc', h_im, c_im.astype(f32))
         + d.astype(f32) * uf)
    return y.reshape(bsz, L, D_SSM), h_re[:, -1], h_im[:, -1]


def _fox_block(q, cq, qpos, k, v, ck, kpos):
    s = jnp.einsum('bqhd,bkhd->bhqk', q, k).astype(jnp.float32) * (HEAD_DIM ** -0.5)
    s = s + (jnp.transpose(cq, (0, 2, 1))[..., :, None] - jnp.transpose(ck, (0, 2, 1))[..., None, :])
    mask = kpos[None, :] <= qpos[:, None]
    s = jnp.where(mask, s, -jnp.inf)
    p = jax.nn.softmax(s, axis=-1)
    return jnp.einsum('bhqk,bkhd->bqhd', p.astype(v.dtype), v)


def _layer(x, params, past):
    (g, w_in, b_f, qg, kg, log_dt, a_re, a_im, b_re, b_im, c_re, c_im, d,
     w_glu, b_glu, ssm_ng, att_ng, w_out) = params
    bsz, L, _ = x.shape
    h = _rmsnorm(x, g)
    proj = h @ w_in
    splits = [D_SSM, 2 * D_SSM, 2 * D_SSM + D_ATT, 2 * D_SSM + 2 * D_ATT,
              2 * D_SSM + 3 * D_ATT, 2 * D_SSM + 4 * D_ATT]
    u, z_ssm, q, k, v, z_att, f_pre = jnp.split(proj, splits, axis=-1)

    if past is None:
        h0_re, h0_im = None, None
    else:
        h0_re, h0_im = past[3], past[4]
    y_ssm, hT_re, hT_im = _s5_scan(u, log_dt, a_re, a_im, b_re, b_im, c_re, c_im, d, h0_re, h0_im)
    zs = jax.nn.gelu(y_ssm, approximate=False)
    y_ssm = zs * jax.nn.sigmoid(zs @ w_glu.astype(jnp.float32) + b_glu.astype(jnp.float32))
    y_ssm = y_ssm * jax.nn.silu(z_ssm.astype(jnp.float32))
    y_ssm = _rmsnorm(y_ssm, ssm_ng).astype(x.dtype)

    q = _rmsnorm(q.reshape(bsz, L, N_HEADS, HEAD_DIM), qg)
    k = _rmsnorm(k.reshape(bsz, L, N_HEADS, HEAD_DIM), kg)
    v = v.reshape(bsz, L, N_HEADS, HEAD_DIM)
    logf = jax.nn.log_sigmoid(f_pre.astype(jnp.float32) + b_f.astype(jnp.float32))
    if past is None:
        k_all, v_all, logf_all = k, v, logf
        offset = 0
    else:
        k_past, v_past, logf_past = past[0], past[1], past[2]
        k_all = jnp.concatenate([k_past, k.astype(k_past.dtype)], axis=1)
        v_all = jnp.concatenate([v_past, v.astype(v_past.dtype)], axis=1)
        logf_all = jnp.concatenate([logf_past.astype(jnp.float32), logf], axis=1)
        offset = k_past.shape[1]
    c_all = jnp.cumsum(logf_all, axis=1)
    cq = c_all[:, offset:]
    S = k_all.shape[1]
    kpos = jnp.arange(S, dtype=jnp.int32)
    qpos = offset + jnp.arange(L, dtype=jnp.int32)
    q = q.astype(k_all.dtype)
    if L <= Q_BLOCK:
        o = _fox_block(q, cq, qpos, k_all, v_all, c_all, kpos)
    else:
        nblk = L // Q_BLOCK
        qb = jnp.transpose(q.reshape(bsz, nblk, Q_BLOCK, N_HEADS, HEAD_DIM), (1, 0, 2, 3, 4))
        cqb = jnp.transpose(cq.reshape(bsz, nblk, Q_BLOCK, N_HEADS), (1, 0, 2, 3))
        qposb = qpos.reshape(nblk, Q_BLOCK)
        ob = lax.map(lambda a: _fox_block(a[0], a[1], a[2], k_all, v_all, c_all, kpos), (qb, cqb, qposb))
        o = jnp.transpose(ob, (1, 0, 2, 3, 4))
    o = o.reshape(bsz, L, D_ATT).astype(jnp.float32)
    y_att = _rmsnorm(o * jax.nn.silu(z_att.astype(jnp.float32)), att_ng).astype(x.dtype)

    y = jnp.concatenate([y_ssm, y_att], axis=-1)
    x = x + (y @ w_out).astype(x.dtype)
    return x, (k, v, logf, hT_re, hT_im)


def setup_inputs(seed: int = 0) -> dict:
    key = jax.random.key(seed)
    ks = jax.random.split(key, 32)
    f32 = jnp.float32
    nrm = lambda k, shape, scale: jax.random.normal(k, shape, f32) * scale
    x_prompt = nrm(ks[0], (BATCH, SEQ, D_MODEL), 1.0)
    x_sample = nrm(ks[1], (DEC_BATCH, DEC_SEQ, D_MODEL), 1.0)
    cache_k = nrm(ks[2], (DEPTH, DEC_BATCH, PAST_LEN, N_HEADS, HEAD_DIM), 1.0)
    cache_v = nrm(ks[3], (DEPTH, DEC_BATCH, PAST_LEN, N_HEADS, HEAD_DIM), 1.0)
    cache_logf = jax.nn.log_sigmoid(F_BIAS_INIT + nrm(ks[4], (DEPTH, DEC_BATCH, PAST_LEN, N_HEADS), 1.0))
    state_ssm_re = nrm(ks[5], (DEPTH, DEC_BATCH, N_GROUPS, STATE_DIM), 0.1)
    state_ssm_im = nrm(ks[6], (DEPTH, DEC_BATCH, N_GROUPS, STATE_DIM), 0.1)
    norm_gain = 1.0 + nrm(ks[7], (DEPTH, D_MODEL), 0.01)
    w_in = nrm(ks[8], (DEPTH, D_MODEL, D_PROJ), D_MODEL ** -0.5)
    b_f = F_BIAS_INIT + nrm(ks[9], (DEPTH, N_HEADS), 0.1)
    q_norm_gain = 1.0 + nrm(ks[10], (DEPTH, HEAD_DIM), 0.01)
    k_norm_gain = 1.0 + nrm(ks[11], (DEPTH, HEAD_DIM), 0.01)
    ssm_log_dt = jax.random.uniform(ks[12], (DEPTH, N_GROUPS), f32, math.log(DT_MIN), math.log(DT_MAX))
    n_idx = jnp.arange(STATE_DIM, dtype=f32)
    ssm_a_re = -0.5 + nrm(ks[13], (DEPTH, N_GROUPS, STATE_DIM), 0.01)
    ssm_a_im = math.pi * n_idx + nrm(ks[14], (DEPTH, N_GROUPS, STATE_DIM), 0.01)
    ssm_b_re = nrm(ks[15], (DEPTH, N_GROUPS, STATE_DIM, SSM_GROUP), (2 * SSM_GROUP) ** -0.5)
    ssm_b_im = nrm(ks[16], (DEPTH, N_GROUPS, STATE_DIM, SSM_GROUP), (2 * SSM_GROUP) ** -0.5)
    ssm_c_re = nrm(ks[17], (DEPTH, N_GROUPS, SSM_GROUP, STATE_DIM), STATE_DIM ** -0.5)
    ssm_c_im = nrm(ks[18], (DEPTH, N_GROUPS, SSM_GROUP, STATE_DIM), STATE_DIM ** -0.5)
    ssm_d = nrm(ks[19], (DEPTH, N_GROUPS, SSM_GROUP), 1.0)
    w_glu = nrm(ks[20], (DEPTH, D_SSM, D_SSM), D_SSM ** -0.5)
    b_glu = nrm(ks[21], (DEPTH, D_SSM), 0.01)
    ssm_out_norm = 1.0 + nrm(ks[22], (DEPTH, D_SSM), 0.01)
    att_out_norm = 1.0 + nrm(ks[23], (DEPTH, D_ATT), 0.01)
    w_out = nrm(ks[24], (DEPTH, D_INNER, D_MODEL), 0.5 * D_INNER ** -0.5)
    return {"x_prompt": x_prompt, "x_sample": x_sample,
            "cache_k": cache_k, "cache_v": cache_v, "cache_logf": cache_logf,
            "state_ssm_re": state_ssm_re, "state_ssm_im": state_ssm_im,
            "norm_gain": norm_gain, "w_in": w_in, "b_f": b_f,
            "q_norm_gain": q_norm_gain, "k_norm_gain": k_norm_gain,
            "ssm_log_dt": ssm_log_dt, "ssm_a_re": ssm_a_re, "ssm_a_im": ssm_a_im,
            "ssm_b_re": ssm_b_re, "ssm_b_im": ssm_b_im, "ssm_c_re": ssm_c_re, "ssm_c_im": ssm_c_im,
            "ssm_d": ssm_d, "w_glu": w_glu, "b_glu": b_glu,
            "ssm_out_norm": ssm_out_norm, "att_out_norm": att_out_norm, "w_out": w_out}


def reference(x_prompt, x_sample, cache_k, cache_v, cache_logf, state_ssm_re, state_ssm_im,
              norm_gain, w_in, b_f, q_norm_gain, k_norm_gain, ssm_log_dt, ssm_a_re, ssm_a_im,
              ssm_b_re, ssm_b_im, ssm_c_re, ssm_c_im, ssm_d, w_glu, b_glu,
              ssm_out_norm, att_out_norm, w_out):
    xp, xs = x_prompt, x_sample
    kp_l, vp_l, fp_l, hrp_l, hip_l = [], [], [], [], []
    ks_l, vs_l, fs_l, hrs_l, his_l = [], [], [], [], []
    for l in range(DEPTH):
        params = (norm_gain[l], w_in[l], b_f[l], q_norm_gain[l], k_norm_gain[l], ssm_log_dt[l],
                  ssm_a_re[l], ssm_a_im[l], ssm_b_re[l], ssm_b_im[l], ssm_c_re[l], ssm_c_im[l],
                  ssm_d[l], w_glu[l], b_glu[l], ssm_out_norm[l], att_out_norm[l], w_out[l])
        xp, (kp, vp, fp, hrp, hip) = _layer(xp, params, None)
        xs, (kk, vv, ff, hrs, his) = _layer(
            xs, params, (cache_k[l], cache_v[l], cache_logf[l], state_ssm_re[l], state_ssm_im[l]))
        kp_l.append(kp); vp_l.append(vp); fp_l.append(fp); hrp_l.append(hrp); hip_l.append(hip)
        ks_l.append(kk); vs_l.append(vv); fs_l.append(ff); hrs_l.append(hrs); his_l.append(his)
    return (xp, xs,
            jnp.stack(kp_l), jnp.stack(vp_l), jnp.stack(fp_l), jnp.stack(hrp_l), jnp.stack(hip_l),
            jnp.stack(ks_l), jnp.stack(vs_l), jnp.stack(fs_l), jnp.stack(hrs_l), jnp.stack(his_l))
```

```python
import functools
import math

import jax
import jax.numpy as jnp
from jax import lax
from jax.experimental import pallas as pl
from jax.experimental.pallas import tpu as pltpu

EPS = 1e-6
LANES = 128
SUBLANES = 8
CHUNK_T = 8
NEG_BIG = -1e30
VMEM_LIMIT = 56 * 1024 * 1024

F32 = jnp.float32
BF16 = jnp.bfloat16
HIGHEST = lax.Precision.HIGHEST


def _dot(a, b):
    return jnp.dot(a, b, preferred_element_type=F32)


def _dot_nt(a, b, precision=None):
    return lax.dot_general(a, b, (((1,), (1,)), ((), ())), precision=precision,
                           preferred_element_type=F32)


def _rms(x, g):
    ms = jnp.mean(x * x, axis=-1, keepdims=True)
    return x * lax.rsqrt(ms + EPS) * g


def _silu(x):
    return x * jax.nn.sigmoid(x)


def _params(sem):
    return pltpu.CompilerParams(dimension_semantics=sem, vmem_limit_bytes=VMEM_LIMIT)


def _s5_prep_kernel(ldt_ref, ar_ref, ai_ref, btr_ref, bti_ref, cr_ref, ci_ref,
                    ws_ref, wm_ref, wc_ref, at_ref, *, gc, p):
    sl = ar_ref.shape[-1]
    reps = sl // p
    dt = jnp.exp(ldt_ref[...])
    ar = ar_ref[...]
    ai = ai_ref[...]

    def power(k):
        mag = jnp.exp(ar * dt * k)
        return mag * jnp.cos(ai * dt * k), mag * jnp.sin(ai * dt * k)

    abr, abi = power(1.0)
    den = ar * ar + ai * ai
    nr = abr - 1.0
    ni = abi
    coef_re = (nr * ar + ni * ai) / den
    coef_im = (ni * ar - nr * ai) / den

    row_g = lax.broadcasted_iota(jnp.int32, (LANES, sl), 0) >> (gc.bit_length() - 1)
    lane_g = lax.broadcasted_iota(jnp.int32, (LANES, sl), 1) >> (p.bit_length() - 1)
    diag = row_g == lane_g

    def expand(ref):
        x = ref[...]
        return jnp.where(diag, jnp.concatenate([x] * reps, axis=1), 0.0)

    b_re, b_im = expand(btr_ref), expand(bti_ref)
    c_re, c_im = expand(cr_ref), expand(ci_ref)
    bb_re = coef_re * b_re - coef_im * b_im
    bb_im = coef_re * b_im + coef_im * b_re

    kt = []
    for k in range(CHUNK_T):
        pr, pi = power(float(k))
        ck_re = c_re * pr - c_im * pi
        ck_im = c_re * pi + c_im * pr
        kt.append(_dot_nt(bb_re, ck_re, HIGHEST) - _dot_nt(bb_im, ck_im, HIGHEST))
    zero = jnp.zeros((LANES, LANES), F32)
    for s in range(CHUNK_T):
        pr, pi = power(float(CHUNK_T - 1 - s))
        rows = slice(s * LANES, (s + 1) * LANES)
        ws_ref[rows, :sl] = (pr * bb_re - pi * bb_im).astype(ws_ref.dtype)
        ws_ref[rows, sl:] = (pr * bb_im + pi * bb_re).astype(ws_ref.dtype)
        for t in range(CHUNK_T):
            blk = kt[t - s] if t >= s else zero
            wm_ref[rows, t * LANES:(t + 1) * LANES] = blk.astype(wm_ref.dtype)
    for k in range(CHUNK_T):
        pr, pi = power(float(k + 1))
        ck_re = c_re * pr - c_im * pi
        ck_im = c_re * pi + c_im * pr
        cols = slice(k * LANES, (k + 1) * LANES)
        wc_ref[:sl, cols] = ck_re.T.astype(wc_ref.dtype)
        wc_ref[sl:, cols] = (-ck_im).T.astype(wc_ref.dtype)
    pr, pi = power(float(CHUNK_T))
    at_ref[:, :sl] = pr
    at_ref[:, sl:] = pi


def _s5_prep(log_dt, a_re, a_im, b_re, b_im, c_re, c_im):
    depth, g, p = a_re.shape
    gc = b_re.shape[-1]
    gl = LANES // gc
    ncb = g // gl
    sl = gl * p
    tw = CHUNK_T * LANES
    lane = lambda x: x.reshape(depth, ncb, 1, sl)
    ldt = lane(jnp.repeat(log_dt, p, axis=-1))
    bt = lambda x: jnp.transpose(x, (0, 1, 3, 2)).reshape(depth, ncb, LANES, p)
    cc = lambda x: x.reshape(depth, ncb, LANES, p)
    vec = pl.BlockSpec((None, None, 1, sl), lambda l, c: (l, c, 0, 0))
    mat = pl.BlockSpec((None, None, LANES, p), lambda l, c: (l, c, 0, 0))
    out = lambda r, w: pl.BlockSpec((None, None, r, w), lambda l, c: (l, c, 0, 0))
    return pl.pallas_call(
        functools.partial(_s5_prep_kernel, gc=gc, p=p),
        grid=(depth, ncb),
        in_specs=[vec, vec, vec, mat, mat, mat, mat],
        out_specs=[out(tw, 2 * sl), out(tw, tw), out(2 * sl, tw), out(1, 2 * sl)],
        out_shape=[jax.ShapeDtypeStruct((depth, ncb, tw, 2 * sl), BF16),
                   jax.ShapeDtypeStruct((depth, ncb, tw, tw), BF16),
                   jax.ShapeDtypeStruct((depth, ncb, 2 * sl, tw), BF16),
                   jax.ShapeDtypeStruct((depth, ncb, 1, 2 * sl), F32)],
        compiler_params=_params(("arbitrary", "arbitrary")),
        name="s5_prep",
    )(ldt, lane(a_re), lane(a_im), bt(b_re), bt(b_im), cc(c_re), cc(c_im))


def _cache_cumsum_kernel(lf_ref, c_ref):
    n = lf_ref.shape[0]
    r = lax.broadcasted_iota(jnp.int32, (LANES, LANES), 0)
    c = lax.broadcasted_iota(jnp.int32, (LANES, LANES), 1)
    tri = (c <= r).astype(F32)
    carry = jnp.zeros((1, lf_ref.shape[1]), F32)
    for blk in range(n // LANES):
        rows = slice(blk * LANES, (blk + 1) * LANES)
        cs = jnp.dot(tri, lf_ref[rows, :], precision=HIGHEST, preferred_element_type=F32) + carry
        c_ref[rows, :] = cs
        carry = cs[LANES - 1:LANES, :]


def _cache_cumsum(cache_logf):
    depth, nb, past, nh = cache_logf.shape
    spec = pl.BlockSpec((None, past, nh), lambda i: (i, 0, 0))
    out = pl.pallas_call(
        _cache_cumsum_kernel,
        grid=(depth * nb,),
        in_specs=[spec], out_specs=spec,
        out_shape=jax.ShapeDtypeStruct((depth * nb, past, nh), F32),
        compiler_params=_params(("arbitrary",)),
        name="cache_cumsum",
    )(cache_logf.reshape(depth * nb, past, nh))
    return out.reshape(depth, nb, past, nh)


def _inproj_kernel(x_ref, g_ref, w_ref, wf_ref, bf_ref, qg_ref, kg_ref, cin_ref,
                   u_ref, zs_ref, q_ref, k_ref, v_ref, za_ref, lf_ref, c_ref,
                   hb_ref, carry_ref, *, tiles_per_seq, stream_rows, nh, hd, ncb):
    i = pl.program_id(0)
    j = pl.program_id(1)
    tm = x_ref.shape[0]

    @pl.when(j == 0)
    def _():
        @pl.when(i % tiles_per_seq == 0)
        def _():
            carry_ref[...] = cin_ref[...]

        hb = _rms(x_ref[...], g_ref[...]).astype(BF16)
        hb_ref[...] = hb
        f = _dot(hb, wf_ref[...]) + bf_ref[...]
        logf = jnp.minimum(f, 0.0) - jnp.log1p(jnp.exp(-jnp.abs(f)))
        r = lax.broadcasted_iota(jnp.int32, (LANES, LANES), 0)
        c = lax.broadcasted_iota(jnp.int32, (LANES, LANES), 1)
        parts = []
        if stream_rows:
            shift = stream_rows.bit_length() - 1
            same = (r >> shift) == (c >> shift)
            tri = (same & (c <= r)).astype(F32)
            ones = same.astype(F32)
            for blk in range(tm // LANES):
                rows = slice(blk * LANES, (blk + 1) * LANES)
                lb = logf[rows, :]
                cin = carry_ref[rows, :]
                parts.append(jnp.dot(tri, lb, precision=HIGHEST, preferred_element_type=F32) + cin)
                carry_ref[rows, :] = jnp.dot(ones, lb, precision=HIGHEST,
                                             preferred_element_type=F32) + cin
        else:
            tri = (c <= r).astype(F32)
            carry = carry_ref[...]
            for blk in range(tm // LANES):
                rows = slice(blk * LANES, (blk + 1) * LANES)
                cs = jnp.dot(tri, logf[rows, :], precision=HIGHEST, preferred_element_type=F32) + carry
                parts.append(cs)
                carry = cs[LANES - 1:LANES, :]
            carry_ref[...] = carry
        csum = jnp.concatenate(parts, axis=0)
        lf_ref[...] = logf[:, :nh].reshape(lf_ref.shape)
        c_ref[...] = csum[:, :nh]

    res = _dot(hb_ref[...], w_ref[...])

    def heads_norm(g):
        return jnp.concatenate(
            [_rms(res[:, h * hd:(h + 1) * hd], g) for h in range(nh)], axis=1)

    @pl.when(j == 0)
    def _():
        for cb in range(ncb):
            u_ref[cb] = res[:, cb * LANES:(cb + 1) * LANES]

    @pl.when(j == 1)
    def _():
        zs_ref[...] = res

    @pl.when(j == 2)
    def _():
        q_ref[...] = (heads_norm(qg_ref[...]) * (hd ** -0.5)).astype(q_ref.dtype)

    @pl.when(j == 3)
    def _():
        k_ref[...] = heads_norm(kg_ref[...]).reshape(k_ref.shape)

    @pl.when(j == 4)
    def _():
        v_ref[...] = res.reshape(v_ref.shape)

    @pl.when(j == 5)
    def _():
        za_ref[...] = res


def _inproj(x, g, w_main, w_f, b_f, qg, kg, c_init, *, tm, tiles_per_seq, stream_rows, nh, q_dtype):
    n, d = x.shape
    sec = w_main.shape[1] // 6
    hd = sec // nh
    ncb = sec // LANES
    nt = n // tm
    row = lambda w: pl.BlockSpec((tm, w), lambda i, j: (i, 0))
    const = lambda a: pl.BlockSpec(a.shape, lambda i, j: (0,) * a.ndim)
    if stream_rows:
        nb = tm // stream_rows
        kv_shape = (nb, nt, stream_rows, sec)
        kv_spec = pl.BlockSpec((nb, None, stream_rows, sec), lambda i, j: (0, i, 0, 0))
        lf_shape = (nb, nt, stream_rows, nh)
        lf_spec = pl.BlockSpec((nb, None, stream_rows, nh), lambda i, j: (0, i, 0, 0))
    else:
        kv_shape, kv_spec = (n, sec), row(sec)
        lf_shape, lf_spec = (n, nh), row(nh)
    return pl.pallas_call(
        functools.partial(_inproj_kernel, tiles_per_seq=tiles_per_seq, stream_rows=stream_rows,
                          nh=nh, hd=hd, ncb=ncb),
        grid=(nt, 6),
        in_specs=[row(d), const(g), pl.BlockSpec((d, sec), lambda i, j: (0, j)),
                  const(w_f), const(b_f), const(qg), const(kg), const(c_init)],
        out_specs=[pl.BlockSpec((ncb, tm, LANES), lambda i, j: (0, i, 0)),
                   row(sec), row(sec), kv_spec, kv_spec, row(sec), lf_spec, row(nh)],
        out_shape=[jax.ShapeDtypeStruct((ncb, n, LANES), F32),
                   jax.ShapeDtypeStruct((n, sec), F32),
                   jax.ShapeDtypeStruct((n, sec), q_dtype),
                   jax.ShapeDtypeStruct(kv_shape, F32),
                   jax.ShapeDtypeStruct(kv_shape, F32),
                   jax.ShapeDtypeStruct((n, sec), F32),
                   jax.ShapeDtypeStruct(lf_shape, F32),
                   jax.ShapeDtypeStruct((n, nh), F32)],
        scratch_shapes=[pltpu.VMEM((tm, d), BF16), pltpu.VMEM(c_init.shape, F32)],
        compiler_params=_params(("arbitrary", "arbitrary")),
        name="inproj",
    )(x, g, w_main, w_f, b_f, qg, kg, c_init)


def _ssm_kernel(u_ref, ws_ref, wm_ref, wc_ref, at_ref, h0_ref, y_ref, ht_ref,
                s_ref, hp_ref, h_ref, *, nb):
    t = pl.program_id(2)
    r = u_ref.shape[0]
    sl = at_ref.shape[-1] // 2

    @pl.when(t == 0)
    def _():
        h_ref[...] = h0_ref[...]

    ub = u_ref[...].astype(BF16)
    s_ref[...] = _dot(ub, ws_ref[...])
    are = at_ref[:, :sl]
    aim = at_ref[:, sl:]

    def step(hre, him, sre, sim):
        return are * hre - aim * him + sre, are * him + aim * hre + sim

    if nb % SUBLANES == 0:
        def body(j, carry):
            hre, him = carry
            rows = pl.ds(pl.multiple_of(j * nb, nb), nb)
            hp_ref[rows, :sl] = hre
            hp_ref[rows, sl:] = him
            return step(hre, him, s_ref[rows, :sl], s_ref[rows, sl:])
        hre, him = lax.fori_loop(0, r // nb, body, (h_ref[:, :sl], h_ref[:, sl:]))
    else:
        assert nb == 1

        def body(j, carry):
            hre, him = carry
            rows = pl.ds(pl.multiple_of(j * SUBLANES, SUBLANES), SUBLANES)
            st = s_ref[rows, :]
            pre, pim = [], []
            for q in range(SUBLANES):
                pre.append(hre)
                pim.append(him)
                hre, him = step(hre, him, st[q:q + 1, :sl], st[q:q + 1, sl:])
            hp_ref[rows, :sl] = jnp.concatenate(pre, axis=0)
            hp_ref[rows, sl:] = jnp.concatenate(pim, axis=0)
            return hre, him
        hre, him = lax.fori_loop(0, r // SUBLANES, body, (h_ref[:, :sl], h_ref[:, sl:]))
    h_ref[:, :sl] = hre
    h_ref[:, sl:] = him
    y_ref[...] = _dot(ub, wm_ref[...]) + _dot(hp_ref[...].astype(BF16), wc_ref[...])

    @pl.when(t == pl.num_programs(2) - 1)
    def _():
        ht_ref[...] = h_ref[...]


def _ssm(u3, ws, wm, wc, at, h0, layer, *, nseq, nb, rt):
    ncb, n, _ = u3.shape
    tw = CHUNK_T * LANES
    nrow = n // CHUNK_T
    nt = nrow // (nseq * rt)
    sl2 = at.shape[-1]
    uc = u3.reshape(ncb, nrow, tw)
    wspec = lambda a: pl.BlockSpec((None, None) + a.shape[2:], lambda c, s, t: (layer, c, 0, 0))
    rows = pl.BlockSpec((None, rt, tw), lambda c, s, t: (c, s * nt + t, 0))
    hspec = pl.BlockSpec((None, None, nb, sl2), lambda c, s, t: (c, s, 0, 0))
    y, ht = pl.pallas_call(
        functools.partial(_ssm_kernel, nb=nb),
        grid=(ncb, nseq, nt),
        in_specs=[rows, wspec(ws), wspec(wm), wspec(wc), wspec(at), hspec],
        out_specs=[rows, hspec],
        out_shape=[jax.ShapeDtypeStruct((ncb, nrow, tw), F32),
                   jax.ShapeDtypeStruct((ncb, nseq, nb, sl2), F32)],
        scratch_shapes=[pltpu.VMEM((rt, sl2), F32), pltpu.VMEM((rt, sl2), F32),
                        pltpu.VMEM((nb, sl2), F32)],
        compiler_params=_params(("arbitrary", "arbitrary", "arbitrary")),
        name="ssm",
    )(uc, ws, wm, wc, at, h0)
    return y.reshape(ncb, n, LANES), ht


def _softmax_update(s, m_old, l_old):
    m_new = jnp.maximum(m_old, jnp.max(s, axis=-1, keepdims=True))
    alpha = jnp.exp(m_old - m_new)
    p = jnp.exp(s - m_new)
    return p, m_new, alpha, alpha * l_old + jnp.sum(p, axis=-1, keepdims=True)


def _attn_prompt_kernel(q_ref, k_ref, v_ref, cq_ref, ck_ref, za_ref, ng_ref, o_ref,
                        m_ref, l_ref, acc_ref, *, nh, hd):
    qi = pl.program_id(1)
    ki = pl.program_id(2)
    tq, tk = q_ref.shape[0], k_ref.shape[0]

    @pl.when(ki == 0)
    def _():
        m_ref[...] = jnp.full(m_ref.shape, NEG_BIG, F32)
        l_ref[...] = jnp.zeros(l_ref.shape, F32)
        acc_ref[...] = jnp.zeros(acc_ref.shape, F32)

    def tile(diagonal):
        if diagonal:
            visible = (lax.broadcasted_iota(jnp.int32, (tq, tk), 1)
                       <= lax.broadcasted_iota(jnp.int32, (tq, tk), 0))
        for h in range(nh):
            cols = slice(h * hd, (h + 1) * hd)
            s = _dot_nt(q_ref[:, cols], k_ref[:, cols].astype(BF16))
            s = s + (cq_ref[:, h:h + 1] - ck_ref[h:h + 1, :])
            if diagonal:
                s = jnp.where(visible, s, NEG_BIG)
            p, m_new, alpha, l_new = _softmax_update(s, m_ref[h], l_ref[h])
            m_ref[h] = m_new
            l_ref[h] = l_new
            acc_ref[:, cols] = alpha * acc_ref[:, cols] + _dot(p.astype(BF16), v_ref[:, cols].astype(BF16))

    @pl.when(ki < qi)
    def _():
        tile(False)

    @pl.when(ki == qi)
    def _():
        tile(True)
        o = jnp.concatenate(
            [acc_ref[:, h * hd:(h + 1) * hd] / l_ref[h] for h in range(nh)], axis=1)
        o_ref[...] = _rms(o * _silu(za_ref[...]), ng_ref[...])


def _attn_prompt(q, k, v, cq, ck_t, za, ng, *, nseq, nh, tq):
    n, da = q.shape
    l = n // nseq
    nq = l // tq
    hd = da // nh
    qrow = lambda w: pl.BlockSpec((tq, w), lambda b, i, j: (b * nq + i, 0))
    krow = pl.BlockSpec((tq, da), lambda b, i, j: (b * nq + jnp.minimum(i, j), 0))
    return pl.pallas_call(
        functools.partial(_attn_prompt_kernel, nh=nh, hd=hd),
        grid=(nseq, nq, nq),
        in_specs=[qrow(da), krow, krow, qrow(nh),
                  pl.BlockSpec((None, nh, tq), lambda b, i, j: (b, 0, jnp.minimum(i, j))),
                  qrow(da), pl.BlockSpec(ng.shape, lambda b, i, j: (0, 0))],
        out_specs=qrow(da),
        out_shape=jax.ShapeDtypeStruct((n, da), F32),
        scratch_shapes=[pltpu.VMEM((nh, tq, 1), F32), pltpu.VMEM((nh, tq, 1), F32),
                        pltpu.VMEM((tq, da), F32)],
        compiler_params=_params(("arbitrary", "arbitrary", "arbitrary")),
        name="attn_prompt",
    )(q, k, v, cq, ck_t, za, ng)


def _attn_sample_kernel(q_ref, kp_ref, vp_ref, kn_ref, vn_ref, cq_ref, ckp_ref, ckn_ref,
                        za_ref, ng_ref, o_ref, *, nh, hd):
    nj, ns, da = q_ref.shape
    tq = nj * ns
    q = q_ref[...].reshape(tq, da).astype(BF16)
    kn = kn_ref[...].reshape(tq, da).astype(BF16)
    vn = vn_ref[...].reshape(tq, da).astype(BF16)
    cq = cq_ref[...].reshape(tq, nh)
    za = za_ref[...].reshape(tq, da)
    visible = (lax.broadcasted_iota(jnp.int32, (tq, tq), 1)
               <= lax.broadcasted_iota(jnp.int32, (tq, tq), 0))
    outs = []
    for h in range(nh):
        cols = slice(h * hd, (h + 1) * hd)
        qh = q[:, cols]
        cqh = cq[:, h:h + 1]
        sp = _dot_nt(qh, kp_ref[:, cols].astype(BF16)) + (cqh - ckp_ref[h:h + 1, :])
        sn = _dot_nt(qh, kn[:, cols]) + (cqh - ckn_ref[h:h + 1, :])
        sn = jnp.where(visible, sn, NEG_BIG)
        m = jnp.maximum(jnp.max(sp, axis=-1, keepdims=True), jnp.max(sn, axis=-1, keepdims=True))
        pp = jnp.exp(sp - m)
        pn = jnp.exp(sn - m)
        l = jnp.sum(pp, axis=-1, keepdims=True) + jnp.sum(pn, axis=-1, keepdims=True)
        o = _dot(pp.astype(BF16), vp_ref[:, cols].astype(BF16)) + _dot(pn.astype(BF16), vn[:, cols])
        outs.append(o / l)
    o = jnp.concatenate(outs, axis=1)
    o_ref[...] = _rms(o * _silu(za), ng_ref[...]).reshape(o_ref.shape)


def _attn_sample(q, k_past, v_past, k_new, v_new, cq, ck_past_t, ck_new_t, za, ng, layer, *, nb, nh):
    n, da = q.shape
    hd = da // nh
    past = k_past.shape[2]
    nj = k_new.shape[1]
    ns = k_new.shape[2]
    view = lambda a: a.reshape(nj, nb, ns, a.shape[-1])
    qspec = lambda w: pl.BlockSpec((nj, None, ns, w), lambda b: (0, b, 0, 0))
    pspec = pl.BlockSpec((None, None, past, da), lambda b: (layer, b, 0, 0))
    nspec = pl.BlockSpec((None, nj, ns, da), lambda b: (b, 0, 0, 0))
    out = pl.pallas_call(
        functools.partial(_attn_sample_kernel, nh=nh, hd=hd),
        grid=(nb,),
        in_specs=[qspec(da), pspec, pspec, nspec, nspec, qspec(nh),
                  pl.BlockSpec((None, None, nh, past), lambda b: (layer, b, 0, 0)),
                  pl.BlockSpec((None, nh, nj * ns), lambda b: (b, 0, 0)),
                  qspec(da), pl.BlockSpec(ng.shape, lambda b: (0, 0))],
        out_specs=qspec(da),
        out_shape=jax.ShapeDtypeStruct((nj, nb, ns, da), F32),
        compiler_params=_params(("arbitrary",)),
        name="attn_sample",
    )(view(q), k_past, v_past, k_new, v_new, view(cq), ck_past_t, ck_new_t, view(za), ng)
    return out.reshape(n, da)


def _out_kernel(x_ref, y_ref, u_ref, zs_ref, ya_ref, d_ref, wg_ref, bg_ref, sng_ref,
                wo_s_ref, wo_a_ref, o_ref, *, ncb):
    ys = jnp.concatenate([y_ref[cb] + d_ref[:, cb * LANES:(cb + 1) * LANES] * u_ref[cb]
                          for cb in range(ncb)], axis=1)
    zs = 0.5 * ys * (1.0 + lax.erf(ys * math.sqrt(0.5)))
    gate = jax.nn.sigmoid(_dot(zs.astype(BF16), wg_ref[...]) + bg_ref[...])
    y_ssm = _rms(zs * gate * _silu(zs_ref[...]), sng_ref[...])
    o_ref[...] = (x_ref[...] + _dot(y_ssm.astype(BF16), wo_s_ref[...])
                  + _dot(ya_ref[...].astype(BF16), wo_a_ref[...]))


def _out_stage(x, y3, u3, zs, ya, d, w_glu, b_glu, sng, wo_s, wo_a, *, tm):
    n, dm = x.shape
    ncb = y3.shape[0]
    ds = zs.shape[1]
    row = lambda w: pl.BlockSpec((tm, w), lambda i: (i, 0))
    blk3 = pl.BlockSpec((ncb, tm, LANES), lambda i: (0, i, 0))
    const = lambda a: pl.BlockSpec(a.shape, lambda i: (0, 0))
    return pl.pallas_call(
        functools.partial(_out_kernel, ncb=ncb),
        grid=(n // tm,),
        in_specs=[row(dm), blk3, blk3, row(ds), row(ya.shape[1]), const(d), const(w_glu),
                  const(b_glu), const(sng), const(wo_s), const(wo_a)],
        out_specs=row(dm),
        out_shape=jax.ShapeDtypeStruct((n, dm), F32),
        compiler_params=_params(("arbitrary",)),
        name="out_stage",
    )(x, y3, u3, zs, ya, d, w_glu, b_glu, sng, wo_s, wo_a)


def _pick(n, pref):
    t = min(n, pref)
    while n % t:
        t //= 2
    return t


def kernel(x_prompt, x_sample, cache_k, cache_v, cache_logf, state_ssm_re, state_ssm_im, norm_gain, w_in, b_f, q_norm_gain, k_norm_gain, ssm_log_dt, ssm_a_re, ssm_a_im, ssm_b_re, ssm_b_im, ssm_c_re, ssm_c_im, ssm_d, w_glu, b_glu, ssm_out_norm, att_out_norm, w_out):
    bsz, seq, dm = x_prompt.shape
    nb, dseq, _ = x_sample.shape
    depth, _, past, nh, hd = cache_k.shape
    _, g, p = ssm_a_re.shape
    d_ssm = w_glu.shape[-1]
    d_att = nh * hd
    assert d_ssm == d_att and w_in.shape[-1] == 2 * d_ssm + 4 * d_att + nh
    assert dseq % CHUNK_T == 0 and seq % CHUNK_T == 0 and nb % SUBLANES == 0
    ncb = d_ssm // LANES
    sl2 = 2 * (LANES // ssm_b_re.shape[-1]) * p
    nj = dseq // CHUNK_T
    n_p, n_s = bsz * seq, nb * dseq

    w_main = w_in[:, :, :6 * d_ssm].astype(BF16)
    w_f = jnp.pad(w_in[:, :, 6 * d_ssm:], ((0, 0), (0, 0), (0, LANES - nh))).astype(BF16)
    b_fp = jnp.pad(b_f, ((0, 0), (0, LANES - nh)))[:, None, :]
    w_glu_b = w_glu.astype(BF16)
    wo_s = w_out[:, :d_ssm].astype(BF16)
    wo_a = w_out[:, d_ssm:].astype(BF16)
    ws, wm, wc, at = _s5_prep(ssm_log_dt, ssm_a_re, ssm_a_im, ssm_b_re, ssm_b_im, ssm_c_re, ssm_c_im)

    c_past = _cache_cumsum(cache_logf)
    ck_past_t = jnp.transpose(c_past, (0, 1, 3, 2))
    c_end = jnp.pad(c_past[:, :, -1, :], ((0, 0), (0, 0), (0, LANES - nh)))
    c_end = jnp.repeat(c_end, CHUNK_T, axis=1)
    k_past = cache_k.reshape(depth, nb, past, d_att)
    v_past = cache_v.reshape(depth, nb, past, d_att)
    to_lanes = lambda a: a.reshape(nb, ncb, sl2 // 2)
    zero_row = jnp.zeros((1, LANES), F32)
    h0_p = jnp.zeros((ncb, bsz, 1, sl2), F32)

    tm_p = _pick(seq, 512)
    tq = _pick(seq, 512)
    rt_p = _pick(seq // CHUNK_T, 256)
    tm_s = nb * CHUNK_T

    xp = x_prompt.reshape(n_p, dm)
    xs = jnp.transpose(x_sample.reshape(nb, nj, CHUNK_T, dm), (1, 0, 2, 3)).reshape(n_s, dm)
    outs = [[] for _ in range(10)]
    for l in range(depth):
        row = lambda a: a[l][None, :]
        common = (row(norm_gain), w_main[l], w_f[l], b_fp[l], row(q_norm_gain), row(k_norm_gain))
        tail = (row(ssm_d.reshape(depth, d_ssm)), w_glu_b[l], row(b_glu), row(ssm_out_norm),
                wo_s[l], wo_a[l])

        u3, zs, q, k, v, za, lf, cs = _inproj(
            xp, *common, zero_row, tm=tm_p, tiles_per_seq=seq // tm_p, stream_rows=0, nh=nh, q_dtype=BF16)
        y3, ht = _ssm(u3, ws, wm, wc, at, h0_p, l, nseq=bsz, nb=1, rt=rt_p)
        ck_t = jnp.transpose(cs.reshape(bsz, seq, nh), (0, 2, 1))
        ya = _attn_prompt(q, k, v, cs, ck_t, za, row(att_out_norm), nseq=bsz, nh=nh, tq=tq)
        xp = _out_stage(xp, y3, u3, zs, ya, *tail, tm=tm_p)
        ht = jnp.transpose(ht[:, :, 0], (1, 0, 2)).reshape(bsz, ncb, 2, g // ncb, p)
        for lst, val in zip(outs[:5], (k, v, lf, ht[:, :, 0].reshape(bsz, g, p), ht[:, :, 1].reshape(bsz, g, p))):
            lst.append(val)

        h0_s = jnp.transpose(jnp.concatenate([to_lanes(state_ssm_re[l]), to_lanes(state_ssm_im[l])], axis=-1),
                             (1, 0, 2))[:, None]
        u3, zs, q, k, v, za, lf, cs = _inproj(
            xs, *common, c_end[l], tm=tm_s, tiles_per_seq=nj, stream_rows=CHUNK_T, nh=nh, q_dtype=F32)
        y3, ht = _ssm(u3, ws, wm, wc, at, h0_s, l, nseq=1, nb=nb, rt=nj * nb)
        ck_new_t = jnp.transpose(cs.reshape(nj, nb, CHUNK_T, nh), (1, 3, 0, 2)).reshape(nb, nh, dseq)
        ya = _attn_sample(q, k_past, v_past, k, v, cs, ck_past_t, ck_new_t, za, row(att_out_norm), l,
                          nb=nb, nh=nh)
        xs = _out_stage(xs, y3, u3, zs, ya, *tail, tm=tm_s)
        ht = jnp.transpose(ht[:, 0], (1, 0, 2)).reshape(nb, ncb, 2, g // ncb, p)
        for lst, val in zip(outs[5:], (k, v, lf, ht[:, :, 0].reshape(nb, g, p), ht[:, :, 1].reshape(nb, g, p))):
            lst.append(val)

    y_prompt = xp.reshape(bsz, seq, dm)
    y_sample = jnp.transpose(xs.reshape(nj, nb, CHUNK_T, dm), (1, 0, 2, 3)).reshape(nb, dseq, dm)
    st = lambda lst, shape: jnp.stack(lst).reshape((depth,) + shape)
    return (y_prompt, y_sample,
            st(outs[0], (bsz, seq, nh, hd)), st(outs[1], (bsz, seq, nh, hd)), st(outs[2], (bsz, seq, nh)),
            st(outs[3], (bsz, g, p)), st(outs[4], (bsz, g, p)),
            st(outs[5], (nb, dseq, nh, hd)), st(outs[6], (nb, dseq, nh, hd)), st(outs[7], (nb, dseq, nh)),
            st(outs[8], (nb, g, p)), st(outs[9], (nb, g, p)))
```

```python
import functools
import math

import jax
import jax.numpy as jnp
from jax import lax
from jax.experimental import pallas as pl
from jax.experimental.pallas import tpu as pltpu

EPS = 1e-6
LANES = 128
SUBLANES = 8
CHUNK_T = 8
NEG_BIG = -1e30
LOG2E = math.log2(math.e)
BIAS_LANES = 8
VMEM_LIMIT = 56 * 1024 * 1024

F32 = jnp.float32
BF16 = jnp.bfloat16
HIGHEST = lax.Precision.HIGHEST


def _dot(a, b):
    return jnp.dot(a, b, preferred_element_type=F32)


def _dot_nt(a, b, precision=None):
    return lax.dot_general(a, b, (((1,), (1,)), ((), ())), precision=precision,
                           preferred_element_type=F32)


def _rms(x, g):
    ms = jnp.mean(x * x, axis=-1, keepdims=True)
    return x * lax.rsqrt(ms + EPS) * g


def _silu(x):
    return x * jax.nn.sigmoid(x)


def _params(sem):
    return pltpu.CompilerParams(dimension_semantics=sem, vmem_limit_bytes=VMEM_LIMIT)


def _split3(x):
    hi = x.astype(BF16)
    r = x - hi.astype(F32)
    mid = r.astype(BF16)
    lo = (r - mid.astype(F32)).astype(BF16)
    return hi, mid, lo


def _s5_prep_kernel(ldt_ref, ar_ref, ai_ref, btr_ref, bti_ref, cr_ref, ci_ref,
                    ws_ref, wm_ref, wc_ref, at_ref, *, gc, p):
    sl = ar_ref.shape[-1]
    reps = sl // p
    dt = jnp.exp(ldt_ref[...])
    ar = ar_ref[...]
    ai = ai_ref[...]

    def power(k):
        mag = jnp.exp(ar * dt * k)
        return mag * jnp.cos(ai * dt * k), mag * jnp.sin(ai * dt * k)

    abr, abi = power(1.0)
    den = ar * ar + ai * ai
    nr = abr - 1.0
    ni = abi
    coef_re = (nr * ar + ni * ai) / den
    coef_im = (ni * ar - nr * ai) / den

    row_g = lax.broadcasted_iota(jnp.int32, (LANES, sl), 0) >> (gc.bit_length() - 1)
    lane_g = lax.broadcasted_iota(jnp.int32, (LANES, sl), 1) >> (p.bit_length() - 1)
    diag = row_g == lane_g

    def expand(ref):
        x = ref[...]
        return jnp.where(diag, jnp.concatenate([x] * reps, axis=1), 0.0)

    b_re, b_im = expand(btr_ref), expand(bti_ref)
    c_re, c_im = expand(cr_ref), expand(ci_ref)
    bb_re = coef_re * b_re - coef_im * b_im
    bb_im = coef_re * b_im + coef_im * b_re

    kt = []
    for k in range(CHUNK_T):
        pr, pi = power(float(k))
        ck_re = c_re * pr - c_im * pi
        ck_im = c_re * pi + c_im * pr
        kt.append(_dot_nt(bb_re, ck_re, HIGHEST) - _dot_nt(bb_im, ck_im, HIGHEST))
    zero = jnp.zeros((LANES, LANES), F32)
    for s in range(CHUNK_T):
        pr, pi = power(float(CHUNK_T - 1 - s))
        rows = slice(s * LANES, (s + 1) * LANES)
        ws_ref[rows, :sl] = (pr * bb_re - pi * bb_im).astype(ws_ref.dtype)
        ws_ref[rows, sl:] = (pr * bb_im + pi * bb_re).astype(ws_ref.dtype)
        for t in range(CHUNK_T):
            blk = kt[t - s] if t >= s else zero
            wm_ref[rows, t * LANES:(t + 1) * LANES] = blk.astype(wm_ref.dtype)
    for k in range(CHUNK_T):
        pr, pi = power(float(k + 1))
        ck_re = c_re * pr - c_im * pi
        ck_im = c_re * pi + c_im * pr
        cols = slice(k * LANES, (k + 1) * LANES)
        wc_ref[:sl, cols] = ck_re.T.astype(wc_ref.dtype)
        wc_ref[sl:, cols] = (-ck_im).T.astype(wc_ref.dtype)
    pr, pi = power(float(CHUNK_T))
    at_ref[:, :sl] = pr
    at_ref[:, sl:] = pi


def _s5_prep(log_dt, a_re, a_im, b_re, b_im, c_re, c_im):
    depth, g, p = a_re.shape
    gc = b_re.shape[-1]
    gl = LANES // gc
    ncb = g // gl
    sl = gl * p
    tw = CHUNK_T * LANES
    lane = lambda x: x.reshape(depth, ncb, 1, sl)
    ldt = lane(jnp.repeat(log_dt, p, axis=-1))
    bt = lambda x: jnp.transpose(x, (0, 1, 3, 2)).reshape(depth, ncb, LANES, p)
    cc = lambda x: x.reshape(depth, ncb, LANES, p)
    vec = pl.BlockSpec((None, None, 1, sl), lambda l, c: (l, c, 0, 0))
    mat = pl.BlockSpec((None, None, LANES, p), lambda l, c: (l, c, 0, 0))
    out = lambda r, w: pl.BlockSpec((None, None, r, w), lambda l, c: (l, c, 0, 0))
    return pl.pallas_call(
        functools.partial(_s5_prep_kernel, gc=gc, p=p),
        grid=(depth, ncb),
        in_specs=[vec, vec, vec, mat, mat, mat, mat],
        out_specs=[out(tw, 2 * sl), out(tw, tw), out(2 * sl, tw), out(1, 2 * sl)],
        out_shape=[jax.ShapeDtypeStruct((depth, ncb, tw, 2 * sl), BF16),
                   jax.ShapeDtypeStruct((depth, ncb, tw, tw), BF16),
                   jax.ShapeDtypeStruct((depth, ncb, 2 * sl, tw), BF16),
                   jax.ShapeDtypeStruct((depth, ncb, 1, 2 * sl), F32)],
        compiler_params=_params(("arbitrary", "arbitrary")),
        name="s5_prep",
    )(ldt, lane(a_re), lane(a_im), bt(b_re), bt(b_im), cc(c_re), cc(c_im))


def _cache_cumsum_kernel(lf_ref, c_ref):
    n = lf_ref.shape[0]
    r = lax.broadcasted_iota(jnp.int32, (LANES, LANES), 0)
    c = lax.broadcasted_iota(jnp.int32, (LANES, LANES), 1)
    tri = (c <= r).astype(F32)
    carry = jnp.zeros((1, lf_ref.shape[1]), F32)
    for blk in range(n // LANES):
        rows = slice(blk * LANES, (blk + 1) * LANES)
        cs = jnp.dot(tri, lf_ref[rows, :], precision=HIGHEST, preferred_element_type=F32) + carry
        c_ref[rows, :] = cs
        carry = cs[LANES - 1:LANES, :]


def _cache_cumsum(cache_logf):
    depth, nb, past, nh = cache_logf.shape
    spec = pl.BlockSpec((None, past, nh), lambda i: (i, 0, 0))
    out = pl.pallas_call(
        _cache_cumsum_kernel,
        grid=(depth * nb,),
        in_specs=[spec], out_specs=spec,
        out_shape=jax.ShapeDtypeStruct((depth * nb, past, nh), F32),
        compiler_params=_params(("arbitrary",)),
        name="cache_cumsum",
    )(cache_logf.reshape(depth * nb, past, nh))
    return out.reshape(depth, nb, past, nh)


def _inproj_kernel(*refs, tiles_per_seq, stream_rows, nh, hd, n_in):
    x_ref, g_ref, w_ref, wf_ref, bf_ref, qg_ref, kg_ref, cin_ref = refs[:8]
    if stream_rows:
        (u_ref, zs_ref, q_ref, kf_ref, vf_ref, za_ref, lf_ref, c_ref,
         hb_ref, carry_ref, cs_ref) = refs[n_in:]
    else:
        (u_ref, zs_ref, q_ref, kf_ref, vf_ref, kb_ref, vt_ref, ek_ref, za_ref, lf_ref,
         hb_ref, carry_ref, cs_ref) = refs[n_in:]
    i = pl.program_id(0)
    j = pl.program_id(1)
    tm = x_ref.shape[0]

    @pl.when(j == 0)
    def _():
        @pl.when(i % tiles_per_seq == 0)
        def _():
            carry_ref[...] = cin_ref[...]

        hb = _rms(x_ref[...], g_ref[...]).astype(BF16)
        hb_ref[...] = hb
        f = _dot(hb, wf_ref[...]) + bf_ref[...]
        logf = jnp.minimum(f, 0.0) - jnp.log1p(jnp.exp(-jnp.abs(f)))
        r = lax.broadcasted_iota(jnp.int32, (LANES, LANES), 0)
        c = lax.broadcasted_iota(jnp.int32, (LANES, LANES), 1)
        if stream_rows:
            shift = stream_rows.bit_length() - 1
            same = (r >> shift) == (c >> shift)
            tri = (same & (c <= r)).astype(F32)
            ones = same.astype(F32)
            for blk in range(tm // LANES):
                rows = slice(blk * LANES, (blk + 1) * LANES)
                lb = logf[rows, :]
                cin = carry_ref[rows, :]
                cs_ref[rows, :] = jnp.dot(tri, lb, precision=HIGHEST, preferred_element_type=F32) + cin
                carry_ref[rows, :] = jnp.dot(ones, lb, precision=HIGHEST,
                                             preferred_element_type=F32) + cin
        else:
            tri = (c <= r).astype(F32)
            carry = carry_ref[...]
            for blk in range(tm // LANES):
                rows = slice(blk * LANES, (blk + 1) * LANES)
                cs = jnp.dot(tri, logf[rows, :], precision=HIGHEST, preferred_element_type=F32) + carry
                cs_ref[rows, :] = cs
                carry = cs[LANES - 1:LANES, :]
            carry_ref[...] = carry
        lf_ref[...] = logf[:, :nh].reshape(lf_ref.shape)
        if stream_rows:
            c_ref[...] = cs_ref[:, :nh]

    res = _dot(hb_ref[...], w_ref[...])

    def heads_norm(g):
        return jnp.concatenate(
            [_rms(res[:, h * hd:(h + 1) * hd], g) for h in range(nh)], axis=1)

    def to_final(ref, val):
        for h in range(nh):
            blk = val[:, h * hd:(h + 1) * hd]
            if stream_rows:
                for b in range(tm // stream_rows):
                    ref[b, pl.ds(h, stream_rows, stride=nh), :] = blk[b * stream_rows:(b + 1) * stream_rows]
            else:
                ref[pl.ds(h, tm, stride=nh), :] = blk

    def bias_block(parts, first):
        r = lax.broadcasted_iota(jnp.int32, (LANES, LANES), 0)
        c = lax.broadcasted_iota(jnp.int32, (LANES, LANES), 1)
        out = None
        for k, part in enumerate(parts):
            sel = ((c == BIAS_LANES * r + first + k) & (r < nh)).astype(BF16)
            term = _dot(part, sel)
            out = term if out is None else out + term
        return out

    lane = lax.broadcasted_iota(jnp.int32, (1, LANES), 1)
    lane_head = lane >> 3
    lane_slot = lane & (BIAS_LANES - 1)

    @pl.when(j == 0)
    def _():
        u_ref[...] = res

    @pl.when(j == 1)
    def _():
        zs_ref[...] = res

    @pl.when(j == 2)
    def _():
        if stream_rows:
            q_ref[...] = heads_norm(qg_ref[...]) * (hd ** -0.5)
        else:
            qn = heads_norm(qg_ref[...]) * (hd ** -0.5 * LOG2E)
            ones = jnp.where((lane_slot >= 3) & (lane_slot < 6) & (lane_head < nh), 1.0, 0.0)
            eq = bias_block(_split3(cs_ref[...] * LOG2E), 0) + ones
            for h in range(nh):
                q_ref[:, 2 * h * hd:(2 * h + 1) * hd] = qn[:, h * hd:(h + 1) * hd].astype(BF16)
                q_ref[:, (2 * h + 1) * hd:(2 * h + 2) * hd] = jnp.where(lane_head == h, eq, 0.0).astype(BF16)

    @pl.when(j == 3)
    def _():
        kn = heads_norm(kg_ref[...])
        to_final(kf_ref, kn)
        if not stream_rows:
            kb_ref[...] = kn.astype(BF16)
            ones = jnp.where((lane_slot < 3) & (lane_head < nh), 1.0, 0.0)
            ek_ref[...] = (ones - bias_block(_split3(cs_ref[...] * LOG2E), 3)).astype(BF16)

    @pl.when(j == 4)
    def _():
        to_final(vf_ref, res)
        if not stream_rows:
            vt_ref[...] = res.T.astype(BF16)

    @pl.when(j == 5)
    def _():
        za_ref[...] = res


def _inproj(x, g, w_main, w_f, b_f, qg, kg, c_init, k_stack, v_stack, layer, depth, *,
            tm, tiles_per_seq, stream_rows, nh):
    n, d = x.shape
    sec = w_main.shape[1] // 6
    hd = sec // nh
    assert hd == LANES and nh * BIAS_LANES <= LANES and tm % LANES == 0
    nt = n // tm
    row = lambda w: pl.BlockSpec((tm, w), lambda i, j: (i, 0))
    const = lambda a: pl.BlockSpec(a.shape, lambda i, j: (0,) * a.ndim)
    shape = jax.ShapeDtypeStruct
    if stream_rows:
        nb = tm // stream_rows
        fin_shape = (depth, nb, nt, stream_rows * nh, hd)
        fin_spec = pl.BlockSpec((None, nb, None, stream_rows * nh, hd), lambda i, j: (layer, 0, i, 0, 0))
        out_specs = [row(sec), row(sec), row(sec), fin_spec, fin_spec, row(sec),
                     pl.BlockSpec((nb, None, stream_rows, nh), lambda i, j: (0, i, 0, 0)), row(nh)]
        out_shape = [shape((n, sec), F32), shape((n, sec), F32), shape((n, sec), F32),
                     shape(fin_shape, F32), shape(fin_shape, F32), shape((n, sec), F32),
                     shape((nb, nt, stream_rows, nh), F32), shape((n, nh), F32)]
    else:
        fin_shape = (depth, n * nh, hd)
        fin_spec = pl.BlockSpec((None, tm * nh, hd), lambda i, j: (layer, i, 0))
        out_specs = [row(sec), row(sec), row(2 * sec), fin_spec, fin_spec, row(sec),
                     pl.BlockSpec((sec, tm), lambda i, j: (0, i)), row(LANES), row(sec), row(nh)]
        out_shape = [shape((n, sec), F32), shape((n, sec), F32), shape((n, 2 * sec), BF16),
                     shape(fin_shape, F32), shape(fin_shape, F32), shape((n, sec), BF16),
                     shape((sec, n), BF16), shape((n, LANES), BF16), shape((n, sec), F32),
                     shape((n, nh), F32)]
    args = [x, g, w_main, w_f, b_f, qg, kg, c_init]
    in_specs = [row(d), const(g), pl.BlockSpec((d, sec), lambda i, j: (0, j)),
                const(w_f), const(b_f), const(qg), const(kg), const(c_init)]
    aliases = {}
    if k_stack is not None:
        args += [k_stack, v_stack]
        in_specs += [pl.BlockSpec(memory_space=pl.ANY)] * 2
        aliases = {8: 3, 9: 4}
    return pl.pallas_call(
        functools.partial(_inproj_kernel, tiles_per_seq=tiles_per_seq, stream_rows=stream_rows,
                          nh=nh, hd=hd, n_in=len(args)),
        grid=(nt, 6),
        in_specs=in_specs, out_specs=out_specs, out_shape=out_shape,
        input_output_aliases=aliases,
        scratch_shapes=[pltpu.VMEM((tm, d), BF16), pltpu.VMEM(c_init.shape, F32),
                        pltpu.VMEM((tm, LANES), F32)],
        compiler_params=_params(("arbitrary", "arbitrary")),
        name="inproj",
    )(*args)


def _ssm_kernel(u_ref, ws_ref, wm_ref, wc_ref, at_ref, h0_ref, y_ref, ht_ref,
                uc_ref, s_ref, hp_ref, h_ref, *, nb):
    t = pl.program_id(2)
    r = uc_ref.shape[0]
    sl = at_ref.shape[-1] // 2
    groups = r // SUBLANES
    span = SUBLANES * CHUNK_T

    def gather(g, _):
        rows = pl.ds(pl.multiple_of(g * SUBLANES, SUBLANES), SUBLANES)
        for s in range(CHUNK_T):
            uc_ref[rows, s * LANES:(s + 1) * LANES] = u_ref[pl.ds(g * span + s, SUBLANES, stride=CHUNK_T), :]
        return 0
    lax.fori_loop(0, groups, gather, 0)

    @pl.when(t == 0)
    def _():
        h_ref[...] = h0_ref[...]

    ub = uc_ref[...].astype(BF16)
    s_ref[...] = _dot(ub, ws_ref[...])
    are = at_ref[:, :sl]
    aim = at_ref[:, sl:]

    def step(hre, him, sre, sim):
        return are * hre - aim * him + sre, are * him + aim * hre + sim

    if nb % SUBLANES == 0:
        def body(j, carry):
            hre, him = carry
            rows = pl.ds(pl.multiple_of(j * nb, nb), nb)
            hp_ref[rows, :sl] = hre
            hp_ref[rows, sl:] = him
            return step(hre, him, s_ref[rows, :sl], s_ref[rows, sl:])
        hre, him = lax.fori_loop(0, r // nb, body, (h_ref[:, :sl], h_ref[:, sl:]))
    else:
        assert nb == 1

        def body(j, carry):
            hre, him = carry
            rows = pl.ds(pl.multiple_of(j * SUBLANES, SUBLANES), SUBLANES)
            st = s_ref[rows, :]
            pre, pim = [], []
            for q in range(SUBLANES):
                pre.append(hre)
                pim.append(him)
                hre, him = step(hre, him, st[q:q + 1, :sl], st[q:q + 1, sl:])
            hp_ref[rows, :sl] = jnp.concatenate(pre, axis=0)
            hp_ref[rows, sl:] = jnp.concatenate(pim, axis=0)
            return hre, him
        hre, him = lax.fori_loop(0, groups, body, (h_ref[:, :sl], h_ref[:, sl:]))
    h_ref[:, :sl] = hre
    h_ref[:, sl:] = him
    uc_ref[...] = _dot(ub, wm_ref[...]) + _dot(hp_ref[...].astype(BF16), wc_ref[...])

    def scatter(g, _):
        rows = pl.ds(pl.multiple_of(g * SUBLANES, SUBLANES), SUBLANES)
        for s in range(CHUNK_T):
            y_ref[pl.ds(g * span + s, SUBLANES, stride=CHUNK_T), :] = uc_ref[rows, s * LANES:(s + 1) * LANES]
        return 0
    lax.fori_loop(0, groups, scatter, 0)

    @pl.when(t == pl.num_programs(2) - 1)
    def _():
        ht_ref[...] = h_ref[...]


def _ssm(u, ws, wm, wc, at, h0, layer, *, nseq, nb, rt):
    n, ds = u.shape
    ncb = ds // LANES
    tw = CHUNK_T * LANES
    nt = n // (CHUNK_T * nseq * rt)
    sl2 = at.shape[-1]
    wspec = lambda a: pl.BlockSpec((None, None) + a.shape[2:], lambda c, s, t: (layer, c, 0, 0))
    rows = pl.BlockSpec((rt * CHUNK_T, LANES), lambda c, s, t: (s * nt + t, c))
    hspec = pl.BlockSpec((None, None, nb, sl2), lambda c, s, t: (c, s, 0, 0))
    return pl.pallas_call(
        functools.partial(_ssm_kernel, nb=nb),
        grid=(ncb, nseq, nt),
        in_specs=[rows, wspec(ws), wspec(wm), wspec(wc), wspec(at), hspec],
        out_specs=[rows, hspec],
        out_shape=[jax.ShapeDtypeStruct((n, ds), F32),
                   jax.ShapeDtypeStruct((ncb, nseq, nb, sl2), F32)],
        scratch_shapes=[pltpu.VMEM((rt, tw), F32), pltpu.VMEM((rt, sl2), F32),
                        pltpu.VMEM((rt, sl2), F32), pltpu.VMEM((nb, sl2), F32)],
        compiler_params=_params(("arbitrary", "arbitrary", "arbitrary")),
        name="ssm",
    )(u, ws, wm, wc, at, h0)


def _attn_prompt_kernel(q_ref, k_ref, ek_ref, vt_ref, za_ref, ng_ref, o_ref, m_ref, acc_ref, *, nh, hd):
    qi = pl.program_id(1)
    ki = pl.program_id(2)
    tq, tk = q_ref.shape[0], k_ref.shape[0]

    @pl.when(ki == 0)
    def _():
        m_ref[...] = jnp.full(m_ref.shape, NEG_BIG, F32)
        acc_ref[...] = jnp.zeros(acc_ref.shape, F32)

    def tile(diagonal):
        ek = ek_ref[...]
        ones = jnp.ones((acc_ref.shape[1] - hd, tk), BF16)
        if diagonal:
            visible = (lax.broadcasted_iota(jnp.int32, (tk, tq), 0)
                       <= lax.broadcasted_iota(jnp.int32, (tk, tq), 1))

        def scores(h):
            ka = jnp.concatenate([k_ref[:, h * hd:(h + 1) * hd], ek], axis=1)
            return _dot_nt(ka, q_ref[:, 2 * h * hd:2 * (h + 1) * hd])

        s_next = scores(0)
        for h in range(nh):
            s = s_next
            if h + 1 < nh:
                s_next = scores(h + 1)
            if diagonal:
                s = jnp.where(visible, s, NEG_BIG)
            m_old = m_ref[h]
            m_new = jnp.maximum(m_old, jnp.max(s, axis=0, keepdims=True))
            p = jnp.exp2(s - m_new).astype(BF16)
            va = jnp.concatenate([vt_ref[h * hd:(h + 1) * hd, :], ones], axis=0)
            acc_ref[h] = jnp.exp2(m_old - m_new) * acc_ref[h] + _dot(va, p)
            m_ref[h] = m_new

    @pl.when(ki < qi)
    def _():
        tile(False)

    @pl.when(ki == qi)
    def _():
        tile(True)
        o = jnp.concatenate([(acc_ref[h, :hd, :] / acc_ref[h, hd:hd + 1, :]).T for h in range(nh)], axis=1)
        o_ref[...] = _rms(o * _silu(za_ref[...]), ng_ref[...])


def _attn_prompt(qa, kb, ek, vt, za, ng, *, nseq, nh, tq):
    n, da = kb.shape
    l = n // nseq
    nq = l // tq
    hd = da // nh
    sum_rows = 2 * SUBLANES
    qrow = lambda w: pl.BlockSpec((tq, w), lambda b, i, j: (b * nq + i, 0))
    krow = lambda w: pl.BlockSpec((tq, w), lambda b, i, j: (b * nq + jnp.minimum(i, j), 0))
    return pl.pallas_call(
        functools.partial(_attn_prompt_kernel, nh=nh, hd=hd),
        grid=(nseq, nq, nq),
        in_specs=[qrow(2 * da), krow(da), krow(LANES),
                  pl.BlockSpec((da, tq), lambda b, i, j: (0, b * nq + jnp.minimum(i, j))),
                  qrow(da), pl.BlockSpec(ng.shape, lambda b, i, j: (0, 0))],
        out_specs=qrow(da),
        out_shape=jax.ShapeDtypeStruct((n, da), F32),
        scratch_shapes=[pltpu.VMEM((nh, 1, tq), F32), pltpu.VMEM((nh, hd + sum_rows, tq), F32)],
        compiler_params=_params(("arbitrary", "arbitrary", "arbitrary")),
        name="attn_prompt",
    )(qa, kb, ek, vt, za, ng)


def _attn_sample_kernel(q_ref, kp_ref, vp_ref, kn_ref, vn_ref, cq_ref, ckp_ref, ckn_ref,
                        za_ref, ng_ref, o_ref, *, nh, hd):
    nj, ns, da = q_ref.shape
    tq = nj * ns
    past = kp_ref.shape[0] // nh
    q = q_ref[...].reshape(tq, da).astype(BF16)
    cq = cq_ref[...].reshape(tq, nh)
    za = za_ref[...].reshape(tq, da)
    visible = (lax.broadcasted_iota(jnp.int32, (tq, tq), 1)
               <= lax.broadcasted_iota(jnp.int32, (tq, tq), 0))
    outs = []
    for h in range(nh):
        head = lambda ref, rows: ref[pl.ds(h, rows, stride=nh), :].astype(BF16)
        qh = q[:, h * hd:(h + 1) * hd]
        cqh = cq[:, h:h + 1]
        sp = _dot_nt(qh, head(kp_ref, past)) + (cqh - ckp_ref[h:h + 1, :])
        sn = _dot_nt(qh, head(kn_ref, tq)) + (cqh - ckn_ref[h:h + 1, :])
        sn = jnp.where(visible, sn, NEG_BIG)
        m = jnp.maximum(jnp.max(sp, axis=-1, keepdims=True), jnp.max(sn, axis=-1, keepdims=True))
        pp = jnp.exp(sp - m)
        pn = jnp.exp(sn - m)
        l = jnp.sum(pp, axis=-1, keepdims=True) + jnp.sum(pn, axis=-1, keepdims=True)
        o = _dot(pp.astype(BF16), head(vp_ref, past)) + _dot(pn.astype(BF16), head(vn_ref, tq))
        outs.append(o / l)
    o = jnp.concatenate(outs, axis=1)
    o_ref[...] = _rms(o * _silu(za), ng_ref[...]).reshape(o_ref.shape)


def _attn_sample(q, k_past, v_past, k_stack, v_stack, cq, ck_past_t, ck_new_t, za, ng, layer, *, nb, nh):
    n, da = q.shape
    hd = da // nh
    depth, _, nj, rows, _ = k_stack.shape
    ns = rows // nh
    past = ck_past_t.shape[-1]
    view = lambda a: a.reshape(nj, nb, ns, a.shape[-1])
    new = lambda a: a.reshape(depth, nb, nj * rows, hd)
    qspec = lambda w: pl.BlockSpec((nj, None, ns, w), lambda b: (0, b, 0, 0))
    pspec = pl.BlockSpec((None, None, past * nh, hd), lambda b: (layer, b, 0, 0))
    nspec = pl.BlockSpec((None, None, nj * rows, hd), lambda b: (layer, b, 0, 0))
    out = pl.pallas_call(
        functools.partial(_attn_sample_kernel, nh=nh, hd=hd),
        grid=(nb,),
        in_specs=[qspec(da), pspec, pspec, nspec, nspec, qspec(nh),
                  pl.BlockSpec((None, None, nh, past), lambda b: (layer, b, 0, 0)),
                  pl.BlockSpec((None, nh, nj * ns), lambda b: (b, 0, 0)),
                  qspec(da), pl.BlockSpec(ng.shape, lambda b: (0, 0))],
        out_specs=qspec(da),
        out_shape=jax.ShapeDtypeStruct((nj, nb, ns, da), F32),
        compiler_params=_params(("arbitrary",)),
        name="attn_sample",
    )(view(q), k_past, v_past, new(k_stack), new(v_stack), view(cq), ck_past_t, ck_new_t, view(za), ng)
    return out.reshape(n, da)


def _out_kernel(x_ref, y_ref, u_ref, zs_ref, ya_ref, d_ref, wg_ref, bg_ref, sng_ref,
                wo_s_ref, wo_a_ref, o_ref):
    ys = y_ref[...] + d_ref[...] * u_ref[...]
    zs = 0.5 * ys * (1.0 + lax.erf(ys * math.sqrt(0.5)))
    gate = jax.nn.sigmoid(_dot(zs.astype(BF16), wg_ref[...]) + bg_ref[...])
    y_ssm = _rms(zs * gate * _silu(zs_ref[...]), sng_ref[...])
    o_ref[...] = (x_ref[...] + _dot(y_ssm.astype(BF16), wo_s_ref[...])
                  + _dot(ya_ref[...].astype(BF16), wo_a_ref[...]))


def _out_stage(x, y, u, zs, ya, d, w_glu, b_glu, sng, wo_s, wo_a, *, tm):
    n, dm = x.shape
    ds = zs.shape[1]
    row = lambda w: pl.BlockSpec((tm, w), lambda i: (i, 0))
    const = lambda a: pl.BlockSpec(a.shape, lambda i: (0, 0))
    return pl.pallas_call(
        _out_kernel,
        grid=(n // tm,),
        in_specs=[row(dm), row(ds), row(ds), row(ds), row(ya.shape[1]), const(d), const(w_glu),
                  const(b_glu), const(sng), const(wo_s), const(wo_a)],
        out_specs=row(dm),
        out_shape=jax.ShapeDtypeStruct((n, dm), F32),
        compiler_params=_params(("arbitrary",)),
        name="out_stage",
    )(x, y, u, zs, ya, d, w_glu, b_glu, sng, wo_s, wo_a)


def _pick(n, pref):
    t = min(n, pref)
    while n % t:
        t //= 2
    return t


def kernel(x_prompt, x_sample, cache_k, cache_v, cache_logf, state_ssm_re, state_ssm_im, norm_gain, w_in, b_f, q_norm_gain, k_norm_gain, ssm_log_dt, ssm_a_re, ssm_a_im, ssm_b_re, ssm_b_im, ssm_c_re, ssm_c_im, ssm_d, w_glu, b_glu, ssm_out_norm, att_out_norm, w_out):
    bsz, seq, dm = x_prompt.shape
    nb, dseq, _ = x_sample.shape
    depth, _, past, nh, hd = cache_k.shape
    _, g, p = ssm_a_re.shape
    d_ssm = w_glu.shape[-1]
    d_att = nh * hd
    assert d_ssm == d_att and w_in.shape[-1] == 2 * d_ssm + 4 * d_att + nh
    assert dseq % CHUNK_T == 0 and seq % CHUNK_T == 0 and nb % SUBLANES == 0
    ncb = d_ssm // LANES
    sl2 = 2 * (LANES // ssm_b_re.shape[-1]) * p
    nj = dseq // CHUNK_T
    n_p, n_s = bsz * seq, nb * dseq

    w_main = w_in[:, :, :6 * d_ssm].astype(BF16)
    w_f = jnp.pad(w_in[:, :, 6 * d_ssm:], ((0, 0), (0, 0), (0, LANES - nh))).astype(BF16)
    b_fp = jnp.pad(b_f, ((0, 0), (0, LANES - nh)))[:, None, :]
    w_glu_b = w_glu.astype(BF16)
    wo_s = w_out[:, :d_ssm].astype(BF16)
    wo_a = w_out[:, d_ssm:].astype(BF16)
    ws, wm, wc, at = _s5_prep(ssm_log_dt, ssm_a_re, ssm_a_im, ssm_b_re, ssm_b_im, ssm_c_re, ssm_c_im)

    c_past = _cache_cumsum(cache_logf)
    ck_past_t = jnp.transpose(c_past, (0, 1, 3, 2))
    c_end = jnp.pad(c_past[:, :, -1, :], ((0, 0), (0, 0), (0, LANES - nh)))
    c_end = jnp.repeat(c_end, CHUNK_T, axis=1)
    k_past = cache_k.reshape(depth, nb, past * nh, hd)
    v_past = cache_v.reshape(depth, nb, past * nh, hd)
    to_lanes = lambda a: a.reshape(nb, ncb, sl2 // 2)
    zero_row = jnp.zeros((1, LANES), F32)
    h0_p = jnp.zeros((ncb, bsz, 1, sl2), F32)

    tm_p = _pick(seq, 512)
    tq = _pick(seq, 512)
    rt_p = _pick(seq // CHUNK_T, 256)
    tm_s = nb * CHUNK_T

    xp = x_prompt.reshape(n_p, dm)
    xs = jnp.transpose(x_sample.reshape(nb, nj, CHUNK_T, dm), (1, 0, 2, 3)).reshape(n_s, dm)
    kp_stack = vp_stack = ks_stack = vs_stack = None
    outs = [[] for _ in range(6)]
    for l in range(depth):
        row = lambda a: a[l][None, :]
        common = (row(norm_gain), w_main[l], w_f[l], b_fp[l], row(q_norm_gain), row(k_norm_gain))
        tail = (row(ssm_d.reshape(depth, d_ssm)), w_glu_b[l], row(b_glu), row(ssm_out_norm),
                wo_s[l], wo_a[l])

        u, zs, qa, kp_stack, vp_stack, kb, vt, ek, za, lf = _inproj(
            xp, *common, zero_row, kp_stack, vp_stack, l, depth,
            tm=tm_p, tiles_per_seq=seq // tm_p, stream_rows=0, nh=nh)
        y, ht = _ssm(u, ws, wm, wc, at, h0_p, l, nseq=bsz, nb=1, rt=rt_p)
        ya = _attn_prompt(qa, kb, ek, vt, za, row(att_out_norm), nseq=bsz, nh=nh, tq=tq)
        xp = _out_stage(xp, y, u, zs, ya, *tail, tm=tm_p)
        ht = jnp.transpose(ht[:, :, 0], (1, 0, 2)).reshape(bsz, ncb, 2, g // ncb, p)
        for lst, val in zip(outs[:3], (lf, ht[:, :, 0].reshape(bsz, g, p), ht[:, :, 1].reshape(bsz, g, p))):
            lst.append(val)

        h0_s = jnp.transpose(jnp.concatenate([to_lanes(state_ssm_re[l]), to_lanes(state_ssm_im[l])], axis=-1),
                             (1, 0, 2))[:, None]
        u, zs, q, ks_stack, vs_stack, za, lf, cs = _inproj(
            xs, *common, c_end[l], ks_stack, vs_stack, l, depth,
            tm=tm_s, tiles_per_seq=nj, stream_rows=CHUNK_T, nh=nh)
        y, ht = _ssm(u, ws, wm, wc, at, h0_s, l, nseq=1, nb=nb, rt=nj * nb)
        ck_new_t = jnp.transpose(cs.reshape(nj, nb, CHUNK_T, nh), (1, 3, 0, 2)).reshape(nb, nh, dseq)
        ya = _attn_sample(q, k_past, v_past, ks_stack, vs_stack, cs, ck_past_t, ck_new_t, za,
                          row(att_out_norm), l, nb=nb, nh=nh)
        xs = _out_stage(xs, y, u, zs, ya, *tail, tm=tm_s)
        ht = jnp.transpose(ht[:, 0], (1, 0, 2)).reshape(nb, ncb, 2, g // ncb, p)
        for lst, val in zip(outs[3:], (lf, ht[:, :, 0].reshape(nb, g, p), ht[:, :, 1].reshape(nb, g, p))):
            lst.append(val)

    y_prompt = xp.reshape(bsz, seq, dm)
    y_sample = jnp.transpose(xs.reshape(nj, nb, CHUNK_T, dm), (1, 0, 2, 3)).reshape(nb, dseq, dm)
    st = lambda lst, shape: jnp.stack(lst).reshape((depth,) + shape)
    return (y_prompt, y_sample,
            kp_stack.reshape(depth, bsz, seq, nh, hd), vp_stack.reshape(depth, bsz, seq, nh, hd),
            st(outs[0], (bsz, seq, nh)), st(outs[1], (bsz, g, p)), st(outs[2], (bsz, g, p)),
            ks_stack.reshape(depth, nb, dseq, nh, hd), vs_stack.reshape(depth, nb, dseq, nh, hd),
            st(outs[3], (nb, dseq, nh)), st(outs[4], (nb, g, p)), st(outs[5], (nb, g, p)))
```

```python
import functools
import math

import jax
import jax.numpy as jnp
from jax import lax
from jax.experimental import pallas as pl
from jax.experimental.pallas import tpu as pltpu

EPS = 1e-6
LANES = 128
SUBLANES = 8
CHUNK_T = 8
NEG_BIG = -1e30
LOG2E = math.log2(math.e)
BIAS_LANES = 8
VMEM_LIMIT = 56 * 1024 * 1024

F32 = jnp.float32
BF16 = jnp.bfloat16
HIGHEST = lax.Precision.HIGHEST


def _dot(a, b):
    return jnp.dot(a, b, preferred_element_type=F32)


def _dot_nt(a, b, precision=None):
    return lax.dot_general(a, b, (((1,), (1,)), ((), ())), precision=precision,
                           preferred_element_type=F32)


def _rms(x, g):
    ms = jnp.mean(x * x, axis=-1, keepdims=True)
    return x * lax.rsqrt(ms + EPS) * g


def _silu(x):
    return x * jax.nn.sigmoid(x)


def _params(sem):
    return pltpu.CompilerParams(dimension_semantics=sem, vmem_limit_bytes=VMEM_LIMIT)


def _layer_spec(a, layer, **kw):
    return pl.BlockSpec((None,) + a.shape[1:], lambda *_: (layer,) + (0,) * (a.ndim - 1), **kw)


def _split3(x):
    hi = x.astype(BF16)
    r = x - hi.astype(F32)
    mid = r.astype(BF16)
    lo = (r - mid.astype(F32)).astype(BF16)
    return hi, mid, lo


def _s5_prep_kernel(ldt_ref, ar_ref, ai_ref, btr_ref, bti_ref, cr_ref, ci_ref,
                    ws_ref, wm_ref, wc_ref, at_ref, *, gc, p):
    sl = ar_ref.shape[-1]
    reps = sl // p
    dt = jnp.exp(ldt_ref[...])
    ar = ar_ref[...]
    ai = ai_ref[...]

    def power(k):
        mag = jnp.exp(ar * dt * k)
        return mag * jnp.cos(ai * dt * k), mag * jnp.sin(ai * dt * k)

    abr, abi = power(1.0)
    den = ar * ar + ai * ai
    nr = abr - 1.0
    ni = abi
    coef_re = (nr * ar + ni * ai) / den
    coef_im = (ni * ar - nr * ai) / den

    row_g = lax.broadcasted_iota(jnp.int32, (LANES, sl), 0) >> (gc.bit_length() - 1)
    lane_g = lax.broadcasted_iota(jnp.int32, (LANES, sl), 1) >> (p.bit_length() - 1)
    diag = row_g == lane_g

    def expand(ref):
        x = ref[...]
        return jnp.where(diag, jnp.concatenate([x] * reps, axis=1), 0.0)

    b_re, b_im = expand(btr_ref), expand(bti_ref)
    c_re, c_im = expand(cr_ref), expand(ci_ref)
    bb_re = coef_re * b_re - coef_im * b_im
    bb_im = coef_re * b_im + coef_im * b_re

    kt = []
    for k in range(CHUNK_T):
        pr, pi = power(float(k))
        ck_re = c_re * pr - c_im * pi
        ck_im = c_re * pi + c_im * pr
        kt.append(_dot_nt(bb_re, ck_re, HIGHEST) - _dot_nt(bb_im, ck_im, HIGHEST))
    zero = jnp.zeros((LANES, LANES), F32)
    for s in range(CHUNK_T):
        pr, pi = power(float(CHUNK_T - 1 - s))
        rows = slice(s * LANES, (s + 1) * LANES)
        ws_ref[rows, :sl] = (pr * bb_re - pi * bb_im).astype(ws_ref.dtype)
        ws_ref[rows, sl:] = (pr * bb_im + pi * bb_re).astype(ws_ref.dtype)
        for t in range(CHUNK_T):
            blk = kt[t - s] if t >= s else zero
            wm_ref[rows, t * LANES:(t + 1) * LANES] = blk.astype(wm_ref.dtype)
    for k in range(CHUNK_T):
        pr, pi = power(float(k + 1))
        ck_re = c_re * pr - c_im * pi
        ck_im = c_re * pi + c_im * pr
        cols = slice(k * LANES, (k + 1) * LANES)
        wc_ref[:sl, cols] = ck_re.T.astype(wc_ref.dtype)
        wc_ref[sl:, cols] = (-ck_im).T.astype(wc_ref.dtype)
    pr, pi = power(float(CHUNK_T))
    at_ref[:, :sl] = pr
    at_ref[:, sl:] = pi


def _s5_prep(log_dt, a_re, a_im, b_re, b_im, c_re, c_im):
    depth, g, p = a_re.shape
    gc = b_re.shape[-1]
    gl = LANES // gc
    ncb = g // gl
    sl = gl * p
    tw = CHUNK_T * LANES
    lane = lambda x: x.reshape(depth, ncb, 1, sl)
    ldt = lane(jnp.repeat(log_dt, p, axis=-1))
    bt = lambda x: jnp.transpose(x, (0, 1, 3, 2)).reshape(depth, ncb, LANES, p)
    cc = lambda x: x.reshape(depth, ncb, LANES, p)
    vec = pl.BlockSpec((None, None, 1, sl), lambda l, c: (l, c, 0, 0))
    mat = pl.BlockSpec((None, None, LANES, p), lambda l, c: (l, c, 0, 0))
    out = lambda r, w: pl.BlockSpec((None, None, r, w), lambda l, c: (l, c, 0, 0))
    return pl.pallas_call(
        functools.partial(_s5_prep_kernel, gc=gc, p=p),
        grid=(depth, ncb),
        in_specs=[vec, vec, vec, mat, mat, mat, mat],
        out_specs=[out(tw, 2 * sl), out(tw, tw), out(2 * sl, tw), out(1, 2 * sl)],
        out_shape=[jax.ShapeDtypeStruct((depth, ncb, tw, 2 * sl), BF16),
                   jax.ShapeDtypeStruct((depth, ncb, tw, tw), BF16),
                   jax.ShapeDtypeStruct((depth, ncb, 2 * sl, tw), BF16),
                   jax.ShapeDtypeStruct((depth, ncb, 1, 2 * sl), F32)],
        compiler_params=_params(("arbitrary", "arbitrary")),
        name="s5_prep",
    )(ldt, lane(a_re), lane(a_im), bt(b_re), bt(b_im), cc(c_re), cc(c_im))


def _cache_cumsum_kernel(lf_ref, c_ref):
    n = lf_ref.shape[0]
    r = lax.broadcasted_iota(jnp.int32, (LANES, LANES), 0)
    c = lax.broadcasted_iota(jnp.int32, (LANES, LANES), 1)
    tri = (c <= r).astype(F32)
    carry = jnp.zeros((1, lf_ref.shape[1]), F32)
    for blk in range(n // LANES):
        rows = slice(blk * LANES, (blk + 1) * LANES)
        cs = jnp.dot(tri, lf_ref[rows, :], precision=HIGHEST, preferred_element_type=F32) + carry
        c_ref[rows, :] = cs
        carry = cs[LANES - 1:LANES, :]


def _cache_cumsum(cache_logf):
    depth, nb, past, nh = cache_logf.shape
    series = depth * nb * nh
    cols = _pick(series, 1024)
    lf_t = jnp.transpose(cache_logf, (2, 0, 1, 3)).reshape(past, series)
    spec = pl.BlockSpec((past, cols), lambda i: (0, i))
    out = pl.pallas_call(
        _cache_cumsum_kernel,
        grid=(series // cols,),
        in_specs=[spec], out_specs=spec,
        out_shape=jax.ShapeDtypeStruct((past, series), F32),
        compiler_params=_params(("arbitrary",)),
        name="cache_cumsum",
    )(lf_t)
    return jnp.transpose(out.reshape(past, depth, nb, nh), (1, 2, 3, 0))


def _inproj_kernel(*refs, tiles_per_seq, stream_rows, nh, hd, n_in):
    x_ref, g_ref, w_ref, wf_ref, bf_ref, qg_ref, kg_ref, cin_ref = refs[:8]
    if stream_rows:
        (u_ref, zs_ref, q_ref, kf_ref, vf_ref, za_ref, lf_ref, c_ref,
         hb_ref, carry_ref, cs_ref) = refs[n_in:]
    else:
        (u_ref, zs_ref, q_ref, kf_ref, vf_ref, kb_ref, vt_ref, ek_ref, za_ref, lf_ref,
         hb_ref, carry_ref, cs_ref) = refs[n_in:]
    i = pl.program_id(0)
    tm = x_ref.shape[0]
    sec = nh * hd

    @pl.when(i % tiles_per_seq == 0)
    def _():
        carry_ref[...] = cin_ref[...]

    hb_ref[...] = _rms(x_ref[...], g_ref[...]).astype(BF16)
    f = _dot(hb_ref[...], wf_ref[...]) + bf_ref[...]
    logf = jnp.minimum(f, 0.0) - jnp.log1p(jnp.exp(-jnp.abs(f)))
    r = lax.broadcasted_iota(jnp.int32, (LANES, LANES), 0)
    c = lax.broadcasted_iota(jnp.int32, (LANES, LANES), 1)
    if stream_rows:
        shift = stream_rows.bit_length() - 1
        same = (r >> shift) == (c >> shift)
        tri = (same & (c <= r)).astype(F32)
        ones = same.astype(F32)
        for blk in range(tm // LANES):
            rows = slice(blk * LANES, (blk + 1) * LANES)
            lb = logf[rows, :]
            cin = carry_ref[rows, :]
            cs_ref[rows, :] = jnp.dot(tri, lb, precision=HIGHEST, preferred_element_type=F32) + cin
            carry_ref[rows, :] = jnp.dot(ones, lb, precision=HIGHEST, preferred_element_type=F32) + cin
    else:
        tri = (c <= r).astype(F32)
        carry = carry_ref[...]
        for blk in range(tm // LANES):
            rows = slice(blk * LANES, (blk + 1) * LANES)
            cs = jnp.dot(tri, logf[rows, :], precision=HIGHEST, preferred_element_type=F32) + carry
            cs_ref[rows, :] = cs
            carry = cs[LANES - 1:LANES, :]
        carry_ref[...] = carry
    lf_ref[...] = logf[:, :nh].reshape(lf_ref.shape)
    if stream_rows:
        c_ref[...] = cs_ref[:, :nh]

    def section(s):
        return _dot(hb_ref[...], w_ref[:, s * sec:(s + 1) * sec])

    def heads_norm(res, g):
        return jnp.concatenate(
            [_rms(res[:, h * hd:(h + 1) * hd], g) for h in range(nh)], axis=1)

    def to_final(ref, val):
        for h in range(nh):
            blk = val[:, h * hd:(h + 1) * hd]
            if stream_rows:
                for b in range(tm // stream_rows):
                    ref[b, pl.ds(h, stream_rows, stride=nh), :] = blk[b * stream_rows:(b + 1) * stream_rows]
            else:
                ref[pl.ds(h, tm, stride=nh), :] = blk

    def bias_block(parts, first):
        r = lax.broadcasted_iota(jnp.int32, (LANES, LANES), 0)
        c = lax.broadcasted_iota(jnp.int32, (LANES, LANES), 1)
        out = None
        for k, part in enumerate(parts):
            sel = ((c == BIAS_LANES * r + first + k) & (r < nh)).astype(BF16)
            term = _dot(part, sel)
            out = term if out is None else out + term
        return out

    lane = lax.broadcasted_iota(jnp.int32, (1, LANES), 1)
    lane_head = lane >> 3
    lane_slot = lane & (BIAS_LANES - 1)

    res = section(0)
    for cb in range(sec // LANES):
        u_ref[cb] = res[:, cb * LANES:(cb + 1) * LANES]
    zs_ref[...] = section(1)

    if stream_rows:
        q_ref[...] = heads_norm(section(2), qg_ref[...]) * (hd ** -0.5)
    else:
        parts = _split3(cs_ref[...] * LOG2E)
        qn = heads_norm(section(2), qg_ref[...]) * (hd ** -0.5 * LOG2E)
        ones = jnp.where((lane_slot >= 3) & (lane_slot < 6) & (lane_head < nh), 1.0, 0.0)
        eq = bias_block(parts, 0) + ones
        for h in range(nh):
            q_ref[:, 2 * h * hd:(2 * h + 1) * hd] = qn[:, h * hd:(h + 1) * hd].astype(BF16)
            q_ref[:, (2 * h + 1) * hd:(2 * h + 2) * hd] = jnp.where(lane_head == h, eq, 0.0).astype(BF16)

    kn = heads_norm(section(3), kg_ref[...])
    to_final(kf_ref, kn)
    if not stream_rows:
        kb_ref[...] = kn.astype(BF16)
        ones = jnp.where((lane_slot < 3) & (lane_head < nh), 1.0, 0.0)
        ek_ref[...] = (ones - bias_block(parts, 3)).astype(BF16)

    res = section(4)
    to_final(vf_ref, res)
    if not stream_rows:
        vt_ref[...] = res.T.astype(BF16)
    za_ref[...] = section(5)


def _inproj(x, g, w_main, w_f, b_f, qg, kg, c_init, k_stack, v_stack, layer, depth, *,
            tm, tiles_per_seq, stream_rows, nh):
    n, d = x.shape
    hd = qg.shape[-1]
    sec = nh * hd
    assert hd == LANES and nh * BIAS_LANES <= LANES and tm % LANES == 0 and w_main.shape[-1] >= 6 * sec
    nt = n // tm
    ncb = sec // LANES
    row = lambda w: pl.BlockSpec((tm, w), lambda i: (i, 0))
    const = lambda a: pl.BlockSpec(a.shape, lambda i: (0,) * a.ndim)
    shape = jax.ShapeDtypeStruct
    u_spec = pl.BlockSpec((ncb, tm, LANES), lambda i: (0, i, 0))
    if stream_rows:
        nb = tm // stream_rows
        fin_shape = (depth, nb, nt, stream_rows * nh, hd)
        fin_spec = pl.BlockSpec((None, nb, None, stream_rows * nh, hd), lambda i: (layer, 0, i, 0, 0))
        out_specs = [u_spec, row(sec), row(sec), fin_spec, fin_spec, row(sec),
                     pl.BlockSpec((nb, None, stream_rows, nh), lambda i: (0, i, 0, 0)), row(nh)]
        out_shape = [shape((ncb, n, LANES), F32), shape((n, sec), F32), shape((n, sec), F32),
                     shape(fin_shape, F32), shape(fin_shape, F32), shape((n, sec), F32),
                     shape((nb, nt, stream_rows, nh), F32), shape((n, nh), F32)]
    else:
        fin_shape = (depth, n * nh, hd)
        fin_spec = pl.BlockSpec((None, tm * nh, hd), lambda i: (layer, i, 0))
        out_specs = [u_spec, row(sec), row(2 * sec), fin_spec, fin_spec, row(sec),
                     pl.BlockSpec((sec, tm), lambda i: (0, i)), row(LANES), row(sec), row(nh)]
        out_shape = [shape((ncb, n, LANES), F32), shape((n, sec), F32), shape((n, 2 * sec), BF16),
                     shape(fin_shape, F32), shape(fin_shape, F32), shape((n, sec), BF16),
                     shape((sec, n), BF16), shape((n, LANES), BF16), shape((n, sec), F32),
                     shape((n, nh), F32)]
    args = [x, g, w_main, w_f, b_f, qg, kg, c_init]
    lspec = lambda a: _layer_spec(a, layer)
    in_specs = [row(d), lspec(g), _layer_spec(w_main, layer, pipeline_mode=pl.Buffered(1)),
                lspec(w_f), lspec(b_f), lspec(qg), lspec(kg), const(c_init)]
    aliases = {}
    if k_stack is not None:
        args += [k_stack, v_stack]
        in_specs += [pl.BlockSpec(memory_space=pl.ANY)] * 2
        aliases = {8: 3, 9: 4}
    return pl.pallas_call(
        functools.partial(_inproj_kernel, tiles_per_seq=tiles_per_seq, stream_rows=stream_rows,
                          nh=nh, hd=hd, n_in=len(args)),
        grid=(nt,),
        in_specs=in_specs, out_specs=out_specs, out_shape=out_shape,
        input_output_aliases=aliases,
        scratch_shapes=[pltpu.VMEM((tm, d), BF16), pltpu.VMEM(c_init.shape, F32),
                        pltpu.VMEM((tm, LANES), F32)],
        compiler_params=_params(("arbitrary",)),
        name="inproj",
    )(*args)


def _ssm_kernel(u_ref, ws_ref, wm_ref, wc_ref, at_ref, h0_ref, y_ref, ht_ref,
                uc_ref, s_ref, hp_ref, h_ref, *, nb):
    t = pl.program_id(2)
    r = uc_ref.shape[0]
    sl = at_ref.shape[-1] // 2
    groups = r // SUBLANES
    span = SUBLANES * CHUNK_T

    def gather(g, _):
        rows = pl.ds(pl.multiple_of(g * SUBLANES, SUBLANES), SUBLANES)
        for s in range(CHUNK_T):
            uc_ref[rows, s * LANES:(s + 1) * LANES] = u_ref[pl.ds(g * span + s, SUBLANES, stride=CHUNK_T), :]
        return 0
    lax.fori_loop(0, groups, gather, 0)

    @pl.when(t == 0)
    def _():
        h_ref[...] = h0_ref[...]

    ub = uc_ref[...].astype(BF16)
    s_ref[...] = _dot(ub, ws_ref[...])
    are = at_ref[:, :sl]
    aim = at_ref[:, sl:]

    def step(hre, him, sre, sim):
        return are * hre - aim * him + sre, are * him + aim * hre + sim

    if nb % SUBLANES == 0:
        def body(j, carry):
            hre, him = carry
            rows = pl.ds(pl.multiple_of(j * nb, nb), nb)
            hp_ref[rows, :sl] = hre
            hp_ref[rows, sl:] = him
            return step(hre, him, s_ref[rows, :sl], s_ref[rows, sl:])
        hre, him = lax.fori_loop(0, r // nb, body, (h_ref[:, :sl], h_ref[:, sl:]))
    else:
        assert nb == 1

        def body(j, carry):
            hre, him = carry
            rows = pl.ds(pl.multiple_of(j * SUBLANES, SUBLANES), SUBLANES)
            st = s_ref[rows, :]
            pre, pim = [], []
            for q in range(SUBLANES):
                pre.append(hre)
                pim.append(him)
                hre, him = step(hre, him, st[q:q + 1, :sl], st[q:q + 1, sl:])
            hp_ref[rows, :sl] = jnp.concatenate(pre, axis=0)
            hp_ref[rows, sl:] = jnp.concatenate(pim, axis=0)
            return hre, him
        hre, him = lax.fori_loop(0, groups, body, (h_ref[:, :sl], h_ref[:, sl:]))
    h_ref[:, :sl] = hre
    h_ref[:, sl:] = him
    uc_ref[...] = _dot(ub, wm_ref[...]) + _dot(hp_ref[...].astype(BF16), wc_ref[...])

    def scatter(g, _):
        rows = pl.ds(pl.multiple_of(g * SUBLANES, SUBLANES), SUBLANES)
        for s in range(CHUNK_T):
            y_ref[pl.ds(g * span + s, SUBLANES, stride=CHUNK_T), :] = uc_ref[rows, s * LANES:(s + 1) * LANES]
        return 0
    lax.fori_loop(0, groups, scatter, 0)

    @pl.when(t == pl.num_programs(2) - 1)
    def _():
        ht_ref[...] = h_ref[...]


def _ssm(u, ws, wm, wc, at, h0, layer, *, nseq, nb, rt):
    ncb, n, _ = u.shape
    tw = CHUNK_T * LANES
    nt = n // (CHUNK_T * nseq * rt)
    sl2 = at.shape[-1]
    wspec = lambda a: pl.BlockSpec((None, None) + a.shape[2:], lambda c, s, t: (layer, c, 0, 0))
    rows = pl.BlockSpec((None, rt * CHUNK_T, LANES), lambda c, s, t: (c, s * nt + t, 0))
    hspec = pl.BlockSpec((None, None, nb, sl2), lambda c, s, t: (c, s, 0, 0))
    return pl.pallas_call(
        functools.partial(_ssm_kernel, nb=nb),
        grid=(ncb, nseq, nt),
        in_specs=[rows, wspec(ws), wspec(wm), wspec(wc), wspec(at), hspec],
        out_specs=[rows, hspec],
        out_shape=[jax.ShapeDtypeStruct((ncb, n, LANES), F32),
                   jax.ShapeDtypeStruct((ncb, nseq, nb, sl2), F32)],
        scratch_shapes=[pltpu.VMEM((rt, tw), F32), pltpu.VMEM((rt, sl2), F32),
                        pltpu.VMEM((rt, sl2), F32), pltpu.VMEM((nb, sl2), F32)],
        compiler_params=_params(("arbitrary", "arbitrary", "arbitrary")),
        name="ssm",
    )(u, ws, wm, wc, at, h0)


def _attn_prompt_kernel(qi_ref, ki_ref, q_ref, k_ref, ek_ref, vt_ref, za_ref, ng_ref, o_ref,
                        m_ref, acc_ref, *, nh, hd):
    qi = qi_ref[pl.program_id(1)]
    ki = ki_ref[pl.program_id(1)]
    tq, tk = q_ref.shape[0], k_ref.shape[0]

    @pl.when(ki == 0)
    def _():
        m_ref[...] = jnp.full(m_ref.shape, NEG_BIG, F32)
        acc_ref[...] = jnp.zeros(acc_ref.shape, F32)

    def tile(diagonal):
        ek = ek_ref[...]
        ones = jnp.ones((acc_ref.shape[1] - hd, tk), BF16)
        if diagonal:
            visible = (lax.broadcasted_iota(jnp.int32, (tk, tq), 0)
                       <= lax.broadcasted_iota(jnp.int32, (tk, tq), 1))

        def scores(h):
            ka = jnp.concatenate([k_ref[:, h * hd:(h + 1) * hd], ek], axis=1)
            return _dot_nt(ka, q_ref[:, 2 * h * hd:2 * (h + 1) * hd])

        s_next = scores(0)
        for h in range(nh):
            s = s_next
            if h + 1 < nh:
                s_next = scores(h + 1)
            if diagonal:
                s = jnp.where(visible, s, NEG_BIG)
            m_old = m_ref[h]
            m_new = jnp.maximum(m_old, jnp.max(s, axis=0, keepdims=True))
            p = jnp.exp2(s - m_new).astype(BF16)
            va = jnp.concatenate([vt_ref[h * hd:(h + 1) * hd, :], ones], axis=0)
            acc_ref[h] = jnp.exp2(m_old - m_new) * acc_ref[h] + _dot(va, p)
            m_ref[h] = m_new

    @pl.when(ki < qi)
    def _():
        tile(False)

    @pl.when(ki == qi)
    def _():
        tile(True)
        o = jnp.concatenate([(acc_ref[h, :hd, :] / acc_ref[h, hd:hd + 1, :]).T for h in range(nh)], axis=1)
        o_ref[...] = _rms(o * _silu(za_ref[...]), ng_ref[...])


def _attn_prompt(qa, kb, ek, vt, za, ng, layer, *, nseq, nh, tq):
    n, da = kb.shape
    l = n // nseq
    nq = l // tq
    hd = da // nh
    sum_rows = 2 * SUBLANES
    pairs = [(i, j) for i in range(nq) for j in range(i + 1)]
    qi_tab = jnp.asarray([i for i, _ in pairs], jnp.int32)
    ki_tab = jnp.asarray([j for _, j in pairs], jnp.int32)
    qrow = lambda w: pl.BlockSpec((tq, w), lambda b, s, qt, kt: (b * nq + qt[s], 0))
    krow = lambda w: pl.BlockSpec((tq, w), lambda b, s, qt, kt: (b * nq + kt[s], 0))
    return pl.pallas_call(
        functools.partial(_attn_prompt_kernel, nh=nh, hd=hd),
        grid_spec=pltpu.PrefetchScalarGridSpec(
            num_scalar_prefetch=2,
            grid=(nseq, len(pairs)),
            in_specs=[qrow(2 * da), krow(da), krow(LANES),
                      pl.BlockSpec((da, tq), lambda b, s, qt, kt: (0, b * nq + kt[s])),
                      qrow(da), _layer_spec(ng, layer)],
            out_specs=qrow(da),
            scratch_shapes=[pltpu.VMEM((nh, 1, tq), F32), pltpu.VMEM((nh, hd + sum_rows, tq), F32)]),
        out_shape=jax.ShapeDtypeStruct((n, da), F32),
        compiler_params=_params(("arbitrary", "arbitrary")),
        name="attn_prompt",
    )(qi_tab, ki_tab, qa, kb, ek, vt, za, ng)


def _attn_sample_kernel(q_ref, kp_ref, vp_ref, kn_ref, vn_ref, cq_ref, ckp_ref, ckn_ref,
                        za_ref, ng_ref, o_ref, *, nh, hd):
    nj, ns, da = q_ref.shape
    tq = nj * ns
    past = kp_ref.shape[0] // nh
    q = q_ref[...].reshape(tq, da).astype(BF16)
    cq = cq_ref[...].reshape(tq, nh)
    za = za_ref[...].reshape(tq, da)
    visible = (lax.broadcasted_iota(jnp.int32, (tq, tq), 1)
               <= lax.broadcasted_iota(jnp.int32, (tq, tq), 0))
    outs = []
    for h in range(nh):
        head = lambda ref, rows: ref[pl.ds(h, rows, stride=nh), :].astype(BF16)
        qh = q[:, h * hd:(h + 1) * hd]
        cqh = cq[:, h:h + 1]
        sp = _dot_nt(qh, head(kp_ref, past)) + (cqh - ckp_ref[h:h + 1, :])
        sn = _dot_nt(qh, head(kn_ref, tq)) + (cqh - ckn_ref[h:h + 1, :])
        sn = jnp.where(visible, sn, NEG_BIG)
        m = jnp.maximum(jnp.max(sp, axis=-1, keepdims=True), jnp.max(sn, axis=-1, keepdims=True))
        pp = jnp.exp(sp - m)
        pn = jnp.exp(sn - m)
        l = jnp.sum(pp, axis=-1, keepdims=True) + jnp.sum(pn, axis=-1, keepdims=True)
        o = _dot(pp.astype(BF16), head(vp_ref, past)) + _dot(pn.astype(BF16), head(vn_ref, tq))
        outs.append(o / l)
    o = jnp.concatenate(outs, axis=1)
    o_ref[...] = _rms(o * _silu(za), ng_ref[...]).reshape(o_ref.shape)


def _attn_sample(q, k_past, v_past, k_stack, v_stack, cq, ck_past_t, ck_new_t, za, ng, layer, *, nb, nh):
    n, da = q.shape
    hd = da // nh
    depth, _, nj, rows, _ = k_stack.shape
    ns = rows // nh
    past = ck_past_t.shape[-1]
    view = lambda a: a.reshape(nj, nb, ns, a.shape[-1])
    new = lambda a: a.reshape(depth, nb, nj * rows, hd)
    qspec = lambda w: pl.BlockSpec((nj, None, ns, w), lambda b: (0, b, 0, 0))
    pspec = pl.BlockSpec((None, None, past * nh, hd), lambda b: (layer, b, 0, 0))
    nspec = pl.BlockSpec((None, None, nj * rows, hd), lambda b: (layer, b, 0, 0))
    out = pl.pallas_call(
        functools.partial(_attn_sample_kernel, nh=nh, hd=hd),
        grid=(nb,),
        in_specs=[qspec(da), pspec, pspec, nspec, nspec, qspec(nh),
                  pl.BlockSpec((None, None, nh, past), lambda b: (layer, b, 0, 0)),
                  pl.BlockSpec((None, nh, nj * ns), lambda b: (b, 0, 0)),
                  qspec(da), _layer_spec(ng, layer)],
        out_specs=qspec(da),
        out_shape=jax.ShapeDtypeStruct((nj, nb, ns, da), F32),
        compiler_params=_params(("arbitrary",)),
        name="attn_sample",
    )(view(q), k_past, v_past, new(k_stack), new(v_stack), view(cq), ck_past_t, ck_new_t, view(za), ng)
    return out.reshape(n, da)


def _out_kernel(x_ref, y_ref, u_ref, zs_ref, ya_ref, d_ref, wg_ref, bg_ref, sng_ref, wo_ref, o_ref):
    ds = zs_ref.shape[1]
    ys = jnp.concatenate([y_ref[cb] + d_ref[:, cb * LANES:(cb + 1) * LANES] * u_ref[cb]
                          for cb in range(y_ref.shape[0])], axis=1)
    zs = 0.5 * ys * (1.0 + lax.erf(ys * math.sqrt(0.5)))
    gate = jax.nn.sigmoid(_dot(zs.astype(BF16), wg_ref[...]) + bg_ref[...])
    y_ssm = _rms(zs * gate * _silu(zs_ref[...]), sng_ref[...])
    o_ref[...] = (x_ref[...] + _dot(y_ssm.astype(BF16), wo_ref[:ds, :])
                  + _dot(ya_ref[...].astype(BF16), wo_ref[ds:, :]))


def _out_stage(x, y, u, zs, ya, d, w_glu, b_glu, sng, w_out, layer, *, tm):
    n, dm = x.shape
    ds = zs.shape[1]
    row = lambda w: pl.BlockSpec((tm, w), lambda i: (i, 0))
    blk3 = pl.BlockSpec((y.shape[0], tm, LANES), lambda i: (0, i, 0))
    lspec = lambda a: _layer_spec(a, layer)
    return pl.pallas_call(
        _out_kernel,
        grid=(n // tm,),
        in_specs=[row(dm), blk3, blk3, row(ds), row(ya.shape[1]), lspec(d), lspec(w_glu),
                  lspec(b_glu), lspec(sng), lspec(w_out)],
        out_specs=row(dm),
        out_shape=jax.ShapeDtypeStruct((n, dm), F32),
        compiler_params=_params(("arbitrary",)),
        name="out_stage",
    )(x, y, u, zs, ya, d, w_glu, b_glu, sng, w_out)


def _pick(n, pref):
    t = min(n, pref)
    while n % t:
        t //= 2
    return t


def kernel(x_prompt, x_sample, cache_k, cache_v, cache_logf, state_ssm_re, state_ssm_im, norm_gain, w_in, b_f, q_norm_gain, k_norm_gain, ssm_log_dt, ssm_a_re, ssm_a_im, ssm_b_re, ssm_b_im, ssm_c_re, ssm_c_im, ssm_d, w_glu, b_glu, ssm_out_norm, att_out_norm, w_out):
    bsz, seq, dm = x_prompt.shape
    nb, dseq, _ = x_sample.shape
    depth, _, past, nh, hd = cache_k.shape
    _, g, p = ssm_a_re.shape
    d_ssm = w_glu.shape[-1]
    d_att = nh * hd
    assert d_ssm == d_att and w_in.shape[-1] == 2 * d_ssm + 4 * d_att + nh
    assert dseq % CHUNK_T == 0 and seq % CHUNK_T == 0 and nb % SUBLANES == 0
    ncb = d_ssm // LANES
    sl2 = 2 * (LANES // ssm_b_re.shape[-1]) * p
    nj = dseq // CHUNK_T
    n_p, n_s = bsz * seq, nb * dseq

    w_main = w_in.astype(BF16)
    w_f = jnp.pad(w_in[:, :, 6 * d_ssm:], ((0, 0), (0, 0), (0, LANES - nh))).astype(BF16)
    b_fp = jnp.pad(b_f, ((0, 0), (0, LANES - nh)))[:, None, :]
    vec = lambda a: a.reshape(depth, 1, -1)
    common = (vec(norm_gain), w_main, w_f, b_fp, vec(q_norm_gain), vec(k_norm_gain))
    tail = (vec(ssm_d), w_glu.astype(BF16), vec(b_glu), vec(ssm_out_norm), w_out.astype(BF16))
    att_ng = vec(att_out_norm)
    ws, wm, wc, at = _s5_prep(ssm_log_dt, ssm_a_re, ssm_a_im, ssm_b_re, ssm_b_im, ssm_c_re, ssm_c_im)

    ck_past_t = _cache_cumsum(cache_logf)
    c_end = jnp.pad(ck_past_t[..., -1], ((0, 0), (0, 0), (0, LANES - nh)))
    c_end = jnp.repeat(c_end, CHUNK_T, axis=1)
    k_past = cache_k.reshape(depth, nb, past * nh, hd)
    v_past = cache_v.reshape(depth, nb, past * nh, hd)
    to_lanes = lambda a: a.reshape(nb, ncb, sl2 // 2)
    zero_row = jnp.zeros((1, LANES), F32)
    h0_p = jnp.zeros((ncb, bsz, 1, sl2), F32)

    tm_p = _pick(seq, 256)
    tq = _pick(seq, 512)
    rt_p = _pick(seq // CHUNK_T, 256)
    tm_s = nb * CHUNK_T

    xp = x_prompt.reshape(n_p, dm)
    xs = jnp.transpose(x_sample.reshape(nb, nj, CHUNK_T, dm), (1, 0, 2, 3)).reshape(n_s, dm)
    kp_stack = vp_stack = ks_stack = vs_stack = None
    outs = [[] for _ in range(6)]
    for l in range(depth):
        u, zs, qa, kp_stack, vp_stack, kb, vt, ek, za, lf = _inproj(
            xp, *common, zero_row, kp_stack, vp_stack, l, depth,
            tm=tm_p, tiles_per_seq=seq // tm_p, stream_rows=0, nh=nh)
        y, ht = _ssm(u, ws, wm, wc, at, h0_p, l, nseq=bsz, nb=1, rt=rt_p)
        ya = _attn_prompt(qa, kb, ek, vt, za, att_ng, l, nseq=bsz, nh=nh, tq=tq)
        xp = _out_stage(xp, y, u, zs, ya, *tail, l, tm=tm_p)
        ht = jnp.transpose(ht[:, :, 0], (1, 0, 2)).reshape(bsz, ncb, 2, g // ncb, p)
        for lst, val in zip(outs[:3], (lf, ht[:, :, 0].reshape(bsz, g, p), ht[:, :, 1].reshape(bsz, g, p))):
            lst.append(val)

        h0_s = jnp.transpose(jnp.concatenate([to_lanes(state_ssm_re[l]), to_lanes(state_ssm_im[l])], axis=-1),
                             (1, 0, 2))[:, None]
        u, zs, q, ks_stack, vs_stack, za, lf, cs = _inproj(
            xs, *common, c_end[l], ks_stack, vs_stack, l, depth,
            tm=tm_s, tiles_per_seq=nj, stream_rows=CHUNK_T, nh=nh)
        y, ht = _ssm(u, ws, wm, wc, at, h0_s, l, nseq=1, nb=nb, rt=nj * nb)
        ck_new_t = jnp.transpose(cs.reshape(nj, nb, CHUNK_T, nh), (1, 3, 0, 2)).reshape(nb, nh, dseq)
        ya = _attn_sample(q, k_past, v_past, ks_stack, vs_stack, cs, ck_past_t, ck_new_t, za,
                          att_ng, l, nb=nb, nh=nh)
        xs = _out_stage(xs, y, u, zs, ya, *tail, l, tm=tm_s)
        ht = jnp.transpose(ht[:, 0], (1, 0, 2)).reshape(nb, ncb, 2, g // ncb, p)
        for lst, val in zip(outs[3:], (lf, ht[:, :, 0].reshape(nb, g, p), ht[:, :, 1].reshape(nb, g, p))):
            lst.append(val)

    y_prompt = xp.reshape(bsz, seq, dm)
    y_sample = jnp.transpose(xs.reshape(nj, nb, CHUNK_T, dm), (1, 0, 2, 3)).reshape(nb, dseq, dm)
    st = lambda lst, shape: jnp.stack(lst).reshape((depth,) + shape)
    return (y_prompt, y_sample,
            kp_stack.reshape(depth, bsz, seq, nh, hd), vp_stack.reshape(depth, bsz, seq, nh, hd),
            st(outs[0], (bsz, seq, nh)), st(outs[1], (bsz, g, p)), st(outs[2], (bsz, g, p)),
            ks_stack.reshape(depth, nb, dseq, nh, hd), vs_stack.reshape(depth, nb, dseq, nh, hd),
            st(outs[3], (nb, dseq, nh)), st(outs[4], (nb, g, p)), st(outs[5], (nb, g, p)))
```

```python
import functools
import math

import jax
import jax.numpy as jnp
from jax import lax
from jax.experimental import pallas as pl
from jax.experimental.pallas import tpu as pltpu

EPS = 1e-6
LANES = 128
SUBLANES = 8
CHUNK_T = 8
NEG_BIG = -1e30
LOG2E = math.log2(math.e)
BIAS_LANES = 8
LOOP_UNROLL = 4
VMEM_LIMIT = 56 * 1024 * 1024

F32 = jnp.float32
BF16 = jnp.bfloat16
HIGHEST = lax.Precision.HIGHEST


def _dot(a, b):
    return jnp.dot(a, b, preferred_element_type=F32)


def _dot_nt(a, b, precision=None):
    return lax.dot_general(a, b, (((1,), (1,)), ((), ())), precision=precision,
                           preferred_element_type=F32)


def _rms(x, g):
    ms = jnp.mean(x * x, axis=-1, keepdims=True)
    return x * lax.rsqrt(ms + EPS) * g


def _silu(x):
    return x * jax.nn.sigmoid(x)


def _params(sem, flags=None):
    return pltpu.CompilerParams(dimension_semantics=sem, vmem_limit_bytes=VMEM_LIMIT, flags=flags)


def _layer_spec(a, layer, **kw):
    return pl.BlockSpec((None,) + a.shape[1:], lambda *_: (layer,) + (0,) * (a.ndim - 1), **kw)


def _split3(x):
    hi = x.astype(BF16)
    r = x - hi.astype(F32)
    mid = r.astype(BF16)
    lo = (r - mid.astype(F32)).astype(BF16)
    return hi, mid, lo


def _s5_prep_kernel(ldt_ref, ar_ref, ai_ref, btr_ref, bti_ref, cr_ref, ci_ref,
                    ws_ref, wm_ref, wc_ref, at_ref, *, gc, p):
    sl = ar_ref.shape[-1]
    reps = sl // p
    dt = jnp.exp(ldt_ref[...])
    ar = ar_ref[...]
    ai = ai_ref[...]

    def power(k):
        mag = jnp.exp(ar * dt * k)
        return mag * jnp.cos(ai * dt * k), mag * jnp.sin(ai * dt * k)

    abr, abi = power(1.0)
    den = ar * ar + ai * ai
    nr = abr - 1.0
    ni = abi
    coef_re = (nr * ar + ni * ai) / den
    coef_im = (ni * ar - nr * ai) / den

    row_g = lax.broadcasted_iota(jnp.int32, (LANES, sl), 0) >> (gc.bit_length() - 1)
    lane_g = lax.broadcasted_iota(jnp.int32, (LANES, sl), 1) >> (p.bit_length() - 1)
    diag = row_g == lane_g

    def expand(ref):
        x = ref[...]
        return jnp.where(diag, jnp.concatenate([x] * reps, axis=1), 0.0)

    b_re, b_im = expand(btr_ref), expand(bti_ref)
    c_re, c_im = expand(cr_ref), expand(ci_ref)
    bb_re = coef_re * b_re - coef_im * b_im
    bb_im = coef_re * b_im + coef_im * b_re

    kt = []
    for k in range(CHUNK_T):
        pr, pi = power(float(k))
        ck_re = c_re * pr - c_im * pi
        ck_im = c_re * pi + c_im * pr
        kt.append(_dot_nt(bb_re, ck_re, HIGHEST) - _dot_nt(bb_im, ck_im, HIGHEST))
    zero = jnp.zeros((LANES, LANES), F32)
    for s in range(CHUNK_T):
        pr, pi = power(float(CHUNK_T - 1 - s))
        rows = slice(s * LANES, (s + 1) * LANES)
        ws_ref[rows, :sl] = (pr * bb_re - pi * bb_im).astype(ws_ref.dtype)
        ws_ref[rows, sl:] = (pr * bb_im + pi * bb_re).astype(ws_ref.dtype)
        for t in range(CHUNK_T):
            blk = kt[t - s] if t >= s else zero
            wm_ref[rows, t * LANES:(t + 1) * LANES] = blk.astype(wm_ref.dtype)
    for k in range(CHUNK_T):
        pr, pi = power(float(k + 1))
        ck_re = c_re * pr - c_im * pi
        ck_im = c_re * pi + c_im * pr
        cols = slice(k * LANES, (k + 1) * LANES)
        wc_ref[:sl, cols] = ck_re.T.astype(wc_ref.dtype)
        wc_ref[sl:, cols] = (-ck_im).T.astype(wc_ref.dtype)
    pr, pi = power(float(CHUNK_T))
    at_ref[:, :sl] = pr
    at_ref[:, sl:] = pi


def _s5_prep(log_dt, a_re, a_im, b_re, b_im, c_re, c_im):
    depth, g, p = a_re.shape
    gc = b_re.shape[-1]
    gl = LANES // gc
    ncb = g // gl
    sl = gl * p
    tw = CHUNK_T * LANES
    lane = lambda x: x.reshape(depth, ncb, 1, sl)
    ldt = lane(jnp.repeat(log_dt, p, axis=-1))
    bt = lambda x: jnp.transpose(x, (0, 1, 3, 2)).reshape(depth, ncb, LANES, p)
    cc = lambda x: x.reshape(depth, ncb, LANES, p)
    vec = pl.BlockSpec((None, None, 1, sl), lambda l, c: (l, c, 0, 0))
    mat = pl.BlockSpec((None, None, LANES, p), lambda l, c: (l, c, 0, 0))
    out = lambda r, w: pl.BlockSpec((None, None, r, w), lambda l, c: (l, c, 0, 0))
    return pl.pallas_call(
        functools.partial(_s5_prep_kernel, gc=gc, p=p),
        grid=(depth, ncb),
        in_specs=[vec, vec, vec, mat, mat, mat, mat],
        out_specs=[out(tw, 2 * sl), out(tw, tw), out(2 * sl, tw), out(1, 2 * sl)],
        out_shape=[jax.ShapeDtypeStruct((depth, ncb, tw, 2 * sl), BF16),
                   jax.ShapeDtypeStruct((depth, ncb, tw, tw), BF16),
                   jax.ShapeDtypeStruct((depth, ncb, 2 * sl, tw), BF16),
                   jax.ShapeDtypeStruct((depth, ncb, 1, 2 * sl), F32)],
        compiler_params=_params(("arbitrary", "arbitrary")),
        name="s5_prep",
    )(ldt, lane(a_re), lane(a_im), bt(b_re), bt(b_im), cc(c_re), cc(c_im))


def _cache_cumsum_kernel(lf_ref, c_ref):
    n = lf_ref.shape[0]
    r = lax.broadcasted_iota(jnp.int32, (LANES, LANES), 0)
    c = lax.broadcasted_iota(jnp.int32, (LANES, LANES), 1)
    tri = (c <= r).astype(F32)
    carry = jnp.zeros((1, lf_ref.shape[1]), F32)
    for blk in range(n // LANES):
        rows = slice(blk * LANES, (blk + 1) * LANES)
        cs = jnp.dot(tri, lf_ref[rows, :], precision=HIGHEST, preferred_element_type=F32) + carry
        c_ref[rows, :] = cs
        carry = cs[LANES - 1:LANES, :]


def _cache_cumsum(cache_logf):
    depth, nb, past, nh = cache_logf.shape
    series = depth * nb * nh
    cols = _pick(series, 1024)
    lf_t = jnp.transpose(cache_logf, (2, 0, 1, 3)).reshape(past, series)
    spec = pl.BlockSpec((past, cols), lambda i: (0, i))
    out = pl.pallas_call(
        _cache_cumsum_kernel,
        grid=(series // cols,),
        in_specs=[spec], out_specs=spec,
        out_shape=jax.ShapeDtypeStruct((past, series), F32),
        compiler_params=_params(("arbitrary",)),
        name="cache_cumsum",
    )(lf_t)
    return jnp.transpose(out.reshape(past, depth, nb, nh), (1, 2, 3, 0))


def _inproj_kernel(*refs, tiles_per_seq, stream_rows, nh, hd, n_in):
    x_ref, g_ref, w_ref, wf_ref, bf_ref, qg_ref, kg_ref, cin_ref = refs[:8]
    if stream_rows:
        (u_ref, zs_ref, q_ref, kf_ref, vf_ref, za_ref, lf_ref, c_ref,
         hb_ref, carry_ref, cs_ref) = refs[n_in:]
    else:
        (u_ref, zs_ref, q_ref, kf_ref, vf_ref, kb_ref, vt_ref, ek_ref, za_ref, lf_ref,
         hb_ref, carry_ref, cs_ref) = refs[n_in:]
    i = pl.program_id(0)
    tm = x_ref.shape[0]
    sec = nh * hd

    @pl.when(i % tiles_per_seq == 0)
    def _():
        carry_ref[...] = cin_ref[...]

    hb_ref[...] = _rms(x_ref[...], g_ref[...]).astype(BF16)
    f = _dot(hb_ref[...], wf_ref[...]) + bf_ref[...]
    logf = jnp.minimum(f, 0.0) - jnp.log1p(jnp.exp(-jnp.abs(f)))
    r = lax.broadcasted_iota(jnp.int32, (LANES, LANES), 0)
    c = lax.broadcasted_iota(jnp.int32, (LANES, LANES), 1)
    if stream_rows:
        shift = stream_rows.bit_length() - 1
        same = (r >> shift) == (c >> shift)
        tri = (same & (c <= r)).astype(F32)
        ones = same.astype(F32)
        for blk in range(tm // LANES):
            rows = slice(blk * LANES, (blk + 1) * LANES)
            lb = logf[rows, :]
            cin = carry_ref[rows, :]
            cs_ref[rows, :] = jnp.dot(tri, lb, precision=HIGHEST, preferred_element_type=F32) + cin
            carry_ref[rows, :] = jnp.dot(ones, lb, precision=HIGHEST, preferred_element_type=F32) + cin
    else:
        tri = (c <= r).astype(F32)
        carry = carry_ref[...]
        for blk in range(tm // LANES):
            rows = slice(blk * LANES, (blk + 1) * LANES)
            cs = jnp.dot(tri, logf[rows, :], precision=HIGHEST, preferred_element_type=F32) + carry
            cs_ref[rows, :] = cs
            carry = cs[LANES - 1:LANES, :]
        carry_ref[...] = carry
    lf_ref[...] = logf[:, :nh].reshape(lf_ref.shape)
    if stream_rows:
        c_ref[...] = cs_ref[:, :nh]

    def section(s):
        return _dot(hb_ref[...], w_ref[:, s * sec:(s + 1) * sec])

    def heads_norm(res, g):
        return jnp.concatenate(
            [_rms(res[:, h * hd:(h + 1) * hd], g) for h in range(nh)], axis=1)

    def to_final(ref, val):
        for h in range(nh):
            blk = val[:, h * hd:(h + 1) * hd]
            if stream_rows:
                for b in range(tm // stream_rows):
                    ref[b, pl.ds(h, stream_rows, stride=nh), :] = blk[b * stream_rows:(b + 1) * stream_rows]
            else:
                ref[pl.ds(h, tm, stride=nh), :] = blk

    def bias_block(parts, first):
        r = lax.broadcasted_iota(jnp.int32, (LANES, LANES), 0)
        c = lax.broadcasted_iota(jnp.int32, (LANES, LANES), 1)
        out = None
        for k, part in enumerate(parts):
            sel = ((c == BIAS_LANES * r + first + k) & (r < nh)).astype(BF16)
            term = _dot(part, sel)
            out = term if out is None else out + term
        return out

    lane = lax.broadcasted_iota(jnp.int32, (1, LANES), 1)
    lane_head = lane >> 3
    lane_slot = lane & (BIAS_LANES - 1)

    res = section(0)
    for cb in range(sec // LANES):
        u_ref[cb] = res[:, cb * LANES:(cb + 1) * LANES]
    zs_ref[...] = section(1)

    if stream_rows:
        q_ref[...] = heads_norm(section(2), qg_ref[...]) * (hd ** -0.5)
    else:
        parts = _split3(cs_ref[...] * LOG2E)
        qn = heads_norm(section(2), qg_ref[...]) * (hd ** -0.5 * LOG2E)
        ones = jnp.where((lane_slot >= 3) & (lane_slot < 6) & (lane_head < nh), 1.0, 0.0)
        eq = bias_block(parts, 0) + ones
        for h in range(nh):
            q_ref[:, 2 * h * hd:(2 * h + 1) * hd] = qn[:, h * hd:(h + 1) * hd].astype(BF16)
            q_ref[:, (2 * h + 1) * hd:(2 * h + 2) * hd] = jnp.where(lane_head == h, eq, 0.0).astype(BF16)

    kn = heads_norm(section(3), kg_ref[...])
    to_final(kf_ref, kn)
    if not stream_rows:
        kb_ref[...] = kn.astype(BF16)
        ones = jnp.where((lane_slot < 3) & (lane_head < nh), 1.0, 0.0)
        ek_ref[...] = (ones - bias_block(parts, 3)).astype(BF16)

    res = section(4)
    to_final(vf_ref, res)
    if not stream_rows:
        vt_ref[...] = res.T.astype(BF16)
    za_ref[...] = section(5)


def _inproj(x, g, w_main, w_f, b_f, qg, kg, c_init, k_stack, v_stack, layer, depth, *,
            tm, tiles_per_seq, stream_rows, nh):
    n, d = x.shape
    hd = qg.shape[-1]
    sec = nh * hd
    assert hd == LANES and nh * BIAS_LANES <= LANES and tm % LANES == 0 and w_main.shape[-1] >= 6 * sec
    nt = n // tm
    ncb = sec // LANES
    row = lambda w: pl.BlockSpec((tm, w), lambda i: (i, 0))
    const = lambda a: pl.BlockSpec(a.shape, lambda i: (0,) * a.ndim)
    shape = jax.ShapeDtypeStruct
    u_spec = pl.BlockSpec((ncb, tm, LANES), lambda i: (0, i, 0))
    if stream_rows:
        nb = tm // stream_rows
        fin_shape = (depth, nb, nt, stream_rows * nh, hd)
        fin_spec = pl.BlockSpec((None, nb, None, stream_rows * nh, hd), lambda i: (layer, 0, i, 0, 0))
        out_specs = [u_spec, row(sec), row(sec), fin_spec, fin_spec, row(sec),
                     pl.BlockSpec((nb, None, stream_rows, nh), lambda i: (0, i, 0, 0)), row(nh)]
        out_shape = [shape((ncb, n, LANES), F32), shape((n, sec), F32), shape((n, sec), F32),
                     shape(fin_shape, F32), shape(fin_shape, F32), shape((n, sec), F32),
                     shape((nb, nt, stream_rows, nh), F32), shape((n, nh), F32)]
    else:
        fin_shape = (depth, n * nh, hd)
        fin_spec = pl.BlockSpec((None, tm * nh, hd), lambda i: (layer, i, 0))
        out_specs = [u_spec, row(sec), row(2 * sec), fin_spec, fin_spec, row(sec),
                     pl.BlockSpec((sec, tm), lambda i: (0, i)), row(LANES), row(sec), row(nh)]
        out_shape = [shape((ncb, n, LANES), F32), shape((n, sec), F32), shape((n, 2 * sec), BF16),
                     shape(fin_shape, F32), shape(fin_shape, F32), shape((n, sec), BF16),
                     shape((sec, n), BF16), shape((n, LANES), BF16), shape((n, sec), F32),
                     shape((n, nh), F32)]
    args = [x, g, w_main, w_f, b_f, qg, kg, c_init]
    lspec = lambda a: _layer_spec(a, layer)
    in_specs = [row(d), lspec(g), _layer_spec(w_main, layer, pipeline_mode=pl.Buffered(1)),
                lspec(w_f), lspec(b_f), lspec(qg), lspec(kg), const(c_init)]
    aliases = {}
    if k_stack is not None:
        args += [k_stack, v_stack]
        in_specs += [pl.BlockSpec(memory_space=pl.ANY)] * 2
        aliases = {8: 3, 9: 4}
    return pl.pallas_call(
        functools.partial(_inproj_kernel, tiles_per_seq=tiles_per_seq, stream_rows=stream_rows,
                          nh=nh, hd=hd, n_in=len(args)),
        grid=(nt,),
        in_specs=in_specs, out_specs=out_specs, out_shape=out_shape,
        input_output_aliases=aliases,
        scratch_shapes=[pltpu.VMEM((tm, d), BF16), pltpu.VMEM(c_init.shape, F32),
                        pltpu.VMEM((tm, LANES), F32)],
        compiler_params=_params(("arbitrary",)),
        name="inproj",
    )(*args)


def _ssm_kernel(u_ref, ws_ref, wm_ref, wc_ref, at_ref, h0_ref, y_ref, ht_ref,
                uc_ref, s_ref, hp_ref, h_ref, *, nb):
    t = pl.program_id(2)
    r = uc_ref.shape[0]
    sl = at_ref.shape[-1] // 2
    groups = r // SUBLANES
    span = SUBLANES * CHUNK_T

    def gather(g, _):
        rows = pl.ds(pl.multiple_of(g * SUBLANES, SUBLANES), SUBLANES)
        for s in range(CHUNK_T):
            uc_ref[rows, s * LANES:(s + 1) * LANES] = u_ref[pl.ds(g * span + s, SUBLANES, stride=CHUNK_T), :]
        return 0
    lax.fori_loop(0, groups, gather, 0, unroll=LOOP_UNROLL)

    @pl.when(t == 0)
    def _():
        h_ref[...] = h0_ref[...]

    ub = uc_ref[...].astype(BF16)
    s_ref[...] = _dot(ub, ws_ref[...])
    are = at_ref[:, :sl]
    aim = at_ref[:, sl:]

    def step(hre, him, sre, sim):
        return are * hre - aim * him + sre, are * him + aim * hre + sim

    if nb % SUBLANES == 0:
        def body(j, carry):
            hre, him = carry
            rows = pl.ds(pl.multiple_of(j * nb, nb), nb)
            hp_ref[rows, :sl] = hre
            hp_ref[rows, sl:] = him
            return step(hre, him, s_ref[rows, :sl], s_ref[rows, sl:])
        hre, him = lax.fori_loop(0, r // nb, body, (h_ref[:, :sl], h_ref[:, sl:]))
    else:
        assert nb == 1

        def body(j, carry):
            hre, him = carry
            rows = pl.ds(pl.multiple_of(j * SUBLANES, SUBLANES), SUBLANES)
            st = s_ref[rows, :]
            pre, pim = [], []
            for q in range(SUBLANES):
                pre.append(hre)
                pim.append(him)
                hre, him = step(hre, him, st[q:q + 1, :sl], st[q:q + 1, sl:])
            hp_ref[rows, :sl] = jnp.concatenate(pre, axis=0)
            hp_ref[rows, sl:] = jnp.concatenate(pim, axis=0)
            return hre, him
        hre, him = lax.fori_loop(0, groups, body, (h_ref[:, :sl], h_ref[:, sl:]))
    h_ref[:, :sl] = hre
    h_ref[:, sl:] = him
    uc_ref[...] = _dot(ub, wm_ref[...]) + _dot(hp_ref[...].astype(BF16), wc_ref[...])

    def scatter(g, _):
        rows = pl.ds(pl.multiple_of(g * SUBLANES, SUBLANES), SUBLANES)
        for s in range(CHUNK_T):
            y_ref[pl.ds(g * span + s, SUBLANES, stride=CHUNK_T), :] = uc_ref[rows, s * LANES:(s + 1) * LANES]
        return 0
    lax.fori_loop(0, groups, scatter, 0, unroll=LOOP_UNROLL)

    @pl.when(t == pl.num_programs(2) - 1)
    def _():
        ht_ref[...] = h_ref[...]


def _ssm(u, ws, wm, wc, at, h0, layer, *, nseq, nb, rt):
    ncb, n, _ = u.shape
    tw = CHUNK_T * LANES
    nt = n // (CHUNK_T * nseq * rt)
    sl2 = at.shape[-1]
    wspec = lambda a: pl.BlockSpec((None, None) + a.shape[2:], lambda c, s, t: (layer, c, 0, 0))
    rows = pl.BlockSpec((None, rt * CHUNK_T, LANES), lambda c, s, t: (c, s * nt + t, 0))
    hspec = pl.BlockSpec((None, None, nb, sl2), lambda c, s, t: (c, s, 0, 0))
    return pl.pallas_call(
        functools.partial(_ssm_kernel, nb=nb),
        grid=(ncb, nseq, nt),
        in_specs=[rows, wspec(ws), wspec(wm), wspec(wc), wspec(at), hspec],
        out_specs=[rows, hspec],
        out_shape=[jax.ShapeDtypeStruct((ncb, n, LANES), F32),
                   jax.ShapeDtypeStruct((ncb, nseq, nb, sl2), F32)],
        scratch_shapes=[pltpu.VMEM((rt, tw), F32), pltpu.VMEM((rt, sl2), F32),
                        pltpu.VMEM((rt, sl2), F32), pltpu.VMEM((nb, sl2), F32)],
        compiler_params=_params(("arbitrary", "arbitrary", "arbitrary")),
        name="ssm",
    )(u, ws, wm, wc, at, h0)


def _attn_prompt_kernel(qi_ref, ki_ref, kind_ref, q_ref, k_ref, ek_ref, vt_ref, za_ref, ng_ref, o_ref,
                        m_ref, acc_ref, *, nh, hd):
    qi = qi_ref[pl.program_id(1)]
    ki = ki_ref[pl.program_id(1)]
    kind = kind_ref[pl.program_id(1)]
    tq, tk = q_ref.shape[0], k_ref.shape[0]

    @pl.when(ki == 0)
    def _():
        m_ref[...] = jnp.full(m_ref.shape, NEG_BIG, F32)
        acc_ref[...] = jnp.zeros(acc_ref.shape, F32)

    def tile(diagonal):
        ek = ek_ref[...]
        ones = jnp.ones((acc_ref.shape[1] - hd, tk), BF16)
        if diagonal:
            visible = (lax.broadcasted_iota(jnp.int32, (tk, tq), 0) + ki * tk
                       <= lax.broadcasted_iota(jnp.int32, (tk, tq), 1) + qi * tq)

        def scores(h):
            ka = jnp.concatenate([k_ref[:, h * hd:(h + 1) * hd], ek], axis=1)
            return _dot_nt(ka, q_ref[:, 2 * h * hd:2 * (h + 1) * hd])

        s_next = scores(0)
        for h in range(nh):
            s = s_next
            if h + 1 < nh:
                s_next = scores(h + 1)
            if diagonal:
                s = jnp.where(visible, s, NEG_BIG)
            m_old = m_ref[h]
            m_new = jnp.maximum(m_old, jnp.max(s, axis=0, keepdims=True))
            p = jnp.exp2(s - m_new).astype(BF16)
            va = jnp.concatenate([vt_ref[h * hd:(h + 1) * hd, :], ones], axis=0)
            acc_ref[h] = jnp.exp2(m_old - m_new) * acc_ref[h] + _dot(va, p)
            m_ref[h] = m_new

    @pl.when(kind == 0)
    def _():
        tile(False)

    @pl.when(kind != 0)
    def _():
        tile(True)

    @pl.when(kind == 2)
    def _():
        o = jnp.concatenate([(acc_ref[h, :hd, :] / acc_ref[h, hd:hd + 1, :]).T for h in range(nh)], axis=1)
        o_ref[...] = _rms(o * _silu(za_ref[...]), ng_ref[...])


def _attn_prompt(qa, kb, ek, vt, za, ng, layer, *, nseq, nh, tq, tk):
    n, da = kb.shape
    l = n // nseq
    nq, nk = l // tq, l // tk
    hd = da // nh
    sum_rows = 2 * SUBLANES
    steps = []
    for i in range(nq):
        last = ((i + 1) * tq - 1) // tk
        for j in range(last + 1):
            steps.append((i, j, 2 if j == last else int((j + 1) * tk - 1 > i * tq)))
    tabs = [jnp.asarray([s[c] for s in steps], jnp.int32) for c in range(3)]
    qrow = lambda w: pl.BlockSpec((tq, w), lambda b, s, qt, kt, kd: (b * nq + qt[s], 0))
    krow = lambda w: pl.BlockSpec((tk, w), lambda b, s, qt, kt, kd: (b * nk + kt[s], 0))
    return pl.pallas_call(
        functools.partial(_attn_prompt_kernel, nh=nh, hd=hd),
        grid_spec=pltpu.PrefetchScalarGridSpec(
            num_scalar_prefetch=3,
            grid=(nseq, len(steps)),
            in_specs=[qrow(2 * da), krow(da), krow(LANES),
                      pl.BlockSpec((da, tk), lambda b, s, qt, kt, kd: (0, b * nk + kt[s])),
                      qrow(da), _layer_spec(ng, layer)],
            out_specs=qrow(da),
            scratch_shapes=[pltpu.VMEM((nh, 1, tq), F32), pltpu.VMEM((nh, hd + sum_rows, tq), F32)]),
        out_shape=jax.ShapeDtypeStruct((n, da), F32),
        compiler_params=_params(("arbitrary", "arbitrary")),
        name="attn_prompt",
    )(*tabs, qa, kb, ek, vt, za, ng)


def _attn_sample_kernel(q_ref, kp_ref, vp_ref, kn_ref, vn_ref, cq_ref, ckp_ref, ckn_ref,
                        za_ref, ng_ref, o_ref, *, nh, hd):
    nj, ns, da = q_ref.shape
    tq = nj * ns
    past = kp_ref.shape[0] // nh
    q = q_ref[...].reshape(tq, da).astype(BF16)
    cq = cq_ref[...].reshape(tq, nh)
    za = za_ref[...].reshape(tq, da)
    visible = (lax.broadcasted_iota(jnp.int32, (tq, tq), 1)
               <= lax.broadcasted_iota(jnp.int32, (tq, tq), 0))
    outs = []
    for h in range(nh):
        head = lambda ref, rows: ref[pl.ds(h, rows, stride=nh), :].astype(BF16)
        qh = q[:, h * hd:(h + 1) * hd]
        cqh = cq[:, h:h + 1]
        sp = _dot_nt(qh, head(kp_ref, past)) + (cqh - ckp_ref[h:h + 1, :])
        sn = _dot_nt(qh, head(kn_ref, tq)) + (cqh - ckn_ref[h:h + 1, :])
        sn = jnp.where(visible, sn, NEG_BIG)
        m = jnp.maximum(jnp.max(sp, axis=-1, keepdims=True), jnp.max(sn, axis=-1, keepdims=True))
        pp = jnp.exp(sp - m)
        pn = jnp.exp(sn - m)
        l = jnp.sum(pp, axis=-1, keepdims=True) + jnp.sum(pn, axis=-1, keepdims=True)
        o = _dot(pp.astype(BF16), head(vp_ref, past)) + _dot(pn.astype(BF16), head(vn_ref, tq))
        outs.append(o / l)
    o = jnp.concatenate(outs, axis=1)
    o_ref[...] = _rms(o * _silu(za), ng_ref[...]).reshape(o_ref.shape)


def _attn_sample(q, k_past, v_past, k_stack, v_stack, cq, ck_past_t, ck_new_t, za, ng, layer, *, nb, nh):
    n, da = q.shape
    hd = da // nh
    depth, _, nj, rows, _ = k_stack.shape
    ns = rows // nh
    past = ck_past_t.shape[-1]
    view = lambda a: a.reshape(nj, nb, ns, a.shape[-1])
    new = lambda a: a.reshape(depth, nb, nj * rows, hd)
    qspec = lambda w: pl.BlockSpec((nj, None, ns, w), lambda b: (0, b, 0, 0))
    pspec = pl.BlockSpec((None, None, past * nh, hd), lambda b: (layer, b, 0, 0))
    nspec = pl.BlockSpec((None, None, nj * rows, hd), lambda b: (layer, b, 0, 0))
    out = pl.pallas_call(
        functools.partial(_attn_sample_kernel, nh=nh, hd=hd),
        grid=(nb,),
        in_specs=[qspec(da), pspec, pspec, nspec, nspec, qspec(nh),
                  pl.BlockSpec((None, None, nh, past), lambda b: (layer, b, 0, 0)),
                  pl.BlockSpec((None, nh, nj * ns), lambda b: (b, 0, 0)),
                  qspec(da), _layer_spec(ng, layer)],
        out_specs=qspec(da),
        out_shape=jax.ShapeDtypeStruct((nj, nb, ns, da), F32),
        compiler_params=_params(("arbitrary",)),
        name="attn_sample",
    )(view(q), k_past, v_past, new(k_stack), new(v_stack), view(cq), ck_past_t, ck_new_t, view(za), ng)
    return out.reshape(n, da)


def _out_kernel(x_ref, y_ref, u_ref, zs_ref, ya_ref, d_ref, wg_ref, bg_ref, sng_ref, wo_ref, o_ref):
    ds = zs_ref.shape[1]
    ys = jnp.concatenate([y_ref[cb] + d_ref[:, cb * LANES:(cb + 1) * LANES] * u_ref[cb]
                          for cb in range(y_ref.shape[0])], axis=1)
    zs = 0.5 * ys * (1.0 + lax.erf(ys * math.sqrt(0.5)))
    gate = jax.nn.sigmoid(_dot(zs.astype(BF16), wg_ref[...]) + bg_ref[...])
    y_ssm = _rms(zs * gate * _silu(zs_ref[...]), sng_ref[...])
    o_ref[...] = (x_ref[...] + _dot(y_ssm.astype(BF16), wo_ref[:ds, :])
                  + _dot(ya_ref[...].astype(BF16), wo_ref[ds:, :]))


def _out_stage(x, y, u, zs, ya, d, w_glu, b_glu, sng, w_out, layer, *, tm):
    n, dm = x.shape
    ds = zs.shape[1]
    row = lambda w: pl.BlockSpec((tm, w), lambda i: (i, 0))
    blk3 = pl.BlockSpec((y.shape[0], tm, LANES), lambda i: (0, i, 0))
    lspec = lambda a: _layer_spec(a, layer)
    return pl.pallas_call(
        _out_kernel,
        grid=(n // tm,),
        in_specs=[row(dm), blk3, blk3, row(ds), row(ya.shape[1]), lspec(d), lspec(w_glu),
                  lspec(b_glu), lspec(sng), lspec(w_out)],
        out_specs=row(dm),
        out_shape=jax.ShapeDtypeStruct((n, dm), F32),
        compiler_params=_params(("arbitrary",)),
        name="out_stage",
    )(x, y, u, zs, ya, d, w_glu, b_glu, sng, w_out)


def _pick(n, pref):
    t = min(n, pref)
    while n % t:
        t //= 2
    return t


def kernel(x_prompt, x_sample, cache_k, cache_v, cache_logf, state_ssm_re, state_ssm_im, norm_gain, w_in, b_f, q_norm_gain, k_norm_gain, ssm_log_dt, ssm_a_re, ssm_a_im, ssm_b_re, ssm_b_im, ssm_c_re, ssm_c_im, ssm_d, w_glu, b_glu, ssm_out_norm, att_out_norm, w_out):
    bsz, seq, dm = x_prompt.shape
    nb, dseq, _ = x_sample.shape
    depth, _, past, nh, hd = cache_k.shape
    _, g, p = ssm_a_re.shape
    d_ssm = w_glu.shape[-1]
    d_att = nh * hd
    assert d_ssm == d_att and w_in.shape[-1] == 2 * d_ssm + 4 * d_att + nh
    assert dseq % CHUNK_T == 0 and seq % CHUNK_T == 0 and nb % SUBLANES == 0
    ncb = d_ssm // LANES
    sl2 = 2 * (LANES // ssm_b_re.shape[-1]) * p
    nj = dseq // CHUNK_T
    n_p, n_s = bsz * seq, nb * dseq

    w_main = w_in.astype(BF16)
    w_f = jnp.pad(w_in[:, :, 6 * d_ssm:], ((0, 0), (0, 0), (0, LANES - nh))).astype(BF16)
    b_fp = jnp.pad(b_f, ((0, 0), (0, LANES - nh)))[:, None, :]
    vec = lambda a: a.reshape(depth, 1, -1)
    common = (vec(norm_gain), w_main, w_f, b_fp, vec(q_norm_gain), vec(k_norm_gain))
    tail = (vec(ssm_d), w_glu.astype(BF16), vec(b_glu), vec(ssm_out_norm), w_out.astype(BF16))
    att_ng = vec(att_out_norm)
    ws, wm, wc, at = _s5_prep(ssm_log_dt, ssm_a_re, ssm_a_im, ssm_b_re, ssm_b_im, ssm_c_re, ssm_c_im)

    ck_past_t = _cache_cumsum(cache_logf)
    c_end = jnp.pad(ck_past_t[..., -1], ((0, 0), (0, 0), (0, LANES - nh)))
    c_end = jnp.repeat(c_end, CHUNK_T, axis=1)
    k_past = cache_k.reshape(depth, nb, past * nh, hd)
    v_past = cache_v.reshape(depth, nb, past * nh, hd)
    to_lanes = lambda a: a.reshape(nb, ncb, sl2 // 2)
    zero_row = jnp.zeros((1, LANES), F32)
    h0_p = jnp.zeros((ncb, bsz, 1, sl2), F32)

    tm_p = _pick(seq, 256)
    tq = _pick(seq, 512)
    tk = _pick(seq, 1024)
    rt_p = _pick(seq // CHUNK_T, 512)
    tm_s = nb * CHUNK_T

    xp = x_prompt.reshape(n_p, dm)
    xs = jnp.transpose(x_sample.reshape(nb, nj, CHUNK_T, dm), (1, 0, 2, 3)).reshape(n_s, dm)
    kp_stack = vp_stack = ks_stack = vs_stack = None
    outs = [[] for _ in range(6)]
    for l in range(depth):
        u, zs, qa, kp_stack, vp_stack, kb, vt, ek, za, lf = _inproj(
            xp, *common, zero_row, kp_stack, vp_stack, l, depth,
            tm=tm_p, tiles_per_seq=seq // tm_p, stream_rows=0, nh=nh)
        y, ht = _ssm(u, ws, wm, wc, at, h0_p, l, nseq=bsz, nb=1, rt=rt_p)
        ya = _attn_prompt(qa, kb, ek, vt, za, att_ng, l, nseq=bsz, nh=nh, tq=tq, tk=tk)
        xp = _out_stage(xp, y, u, zs, ya, *tail, l, tm=tm_p)
        ht = jnp.transpose(ht[:, :, 0], (1, 0, 2)).reshape(bsz, ncb, 2, g // ncb, p)
        for lst, val in zip(outs[:3], (lf, ht[:, :, 0].reshape(bsz, g, p), ht[:, :, 1].reshape(bsz, g, p))):
            lst.append(val)

        h0_s = jnp.transpose(jnp.concatenate([to_lanes(state_ssm_re[l]), to_lanes(state_ssm_im[l])], axis=-1),
                             (1, 0, 2))[:, None]
        u, zs, q, ks_stack, vs_stack, za, lf, cs = _inproj(
            xs, *common, c_end[l], ks_stack, vs_stack, l, depth,
            tm=tm_s, tiles_per_seq=nj, stream_rows=CHUNK_T, nh=nh)
        y, ht = _ssm(u, ws, wm, wc, at, h0_s, l, nseq=1, nb=nb, rt=nj * nb)
        ck_new_t = jnp.transpose(cs.reshape(nj, nb, CHUNK_T, nh), (1, 3, 0, 2)).reshape(nb, nh, dseq)
        ya = _attn_sample(q, k_past, v_past, ks_stack, vs_stack, cs, ck_past_t, ck_new_t, za,
                          att_ng, l, nb=nb, nh=nh)
        xs = _out_stage(xs, y, u, zs, ya, *tail, l, tm=tm_s)
        ht = jnp.transpose(ht[:, 0], (1, 0, 2)).reshape(nb, ncb, 2, g // ncb, p)
        for lst, val in zip(outs[3:], (lf, ht[:, :, 0].reshape(nb, g, p), ht[:, :, 1].reshape(nb, g, p))):
            lst.append(val)

    y_prompt = xp.reshape(bsz, seq, dm)
    y_sample = jnp.transpose(xs.reshape(nj, nb, CHUNK_T, dm), (1, 0, 2, 3)).reshape(nb, dseq, dm)
    st = lambda lst, shape: jnp.stack(lst).reshape((depth,) + shape)
    return (y_prompt, y_sample,
            kp_stack.reshape(depth, bsz, seq, nh, hd), vp_stack.reshape(depth, bsz, seq, nh, hd),
            st(outs[0], (bsz, seq, nh)), st(outs[1], (bsz, g, p)), st(outs[2], (bsz, g, p)),
            ks_stack.reshape(depth, nb, dseq, nh, hd), vs_stack.reshape(depth, nb, dseq, nh, hd),
            st(outs[3], (nb, dseq, nh)), st(outs[4], (nb, g, p)), st(outs[5], (nb, g, p)))
```

```python
import functools
import math

import jax
import jax.numpy as jnp
from jax import lax
from jax.experimental import pallas as pl
from jax.experimental.pallas import tpu as pltpu

EPS = 1e-6
LANES = 128
SUBLANES = 8
CHUNK_T = 8
NEG_BIG = -1e30
LOG2E = math.log2(math.e)
BIAS_LANES = 8
LOOP_UNROLL = 4
VMEM_LIMIT = 56 * 1024 * 1024

F32 = jnp.float32
BF16 = jnp.bfloat16
HIGHEST = lax.Precision.HIGHEST


def _dot(a, b):
    return jnp.dot(a, b, preferred_element_type=F32)


def _dot_nt(a, b, precision=None):
    return lax.dot_general(a, b, (((1,), (1,)), ((), ())), precision=precision,
                           preferred_element_type=F32)


def _rms(x, g):
    ms = jnp.mean(x * x, axis=-1, keepdims=True)
    return x * lax.rsqrt(ms + EPS) * g


def _silu(x):
    return x * jax.nn.sigmoid(x)


def _params(sem, flags=None):
    return pltpu.CompilerParams(dimension_semantics=sem, vmem_limit_bytes=VMEM_LIMIT, flags=flags)


def _layer_spec(a, layer, **kw):
    return pl.BlockSpec((None,) + a.shape[1:], lambda *_: (layer,) + (0,) * (a.ndim - 1), **kw)


def _split3(x):
    hi = x.astype(BF16)
    r = x - hi.astype(F32)
    mid = r.astype(BF16)
    lo = (r - mid.astype(F32)).astype(BF16)
    return hi, mid, lo


def _s5_prep_kernel(ldt_ref, ar_ref, ai_ref, btr_ref, bti_ref, cr_ref, ci_ref,
                    ws_ref, wm_ref, wc_ref, at_ref, *, gc, p):
    sl = ar_ref.shape[-1]
    reps = sl // p
    dt = jnp.exp(ldt_ref[...])
    ar = ar_ref[...]
    ai = ai_ref[...]

    def power(k):
        mag = jnp.exp(ar * dt * k)
        return mag * jnp.cos(ai * dt * k), mag * jnp.sin(ai * dt * k)

    abr, abi = power(1.0)
    den = ar * ar + ai * ai
    nr = abr - 1.0
    ni = abi
    coef_re = (nr * ar + ni * ai) / den
    coef_im = (ni * ar - nr * ai) / den

    row_g = lax.broadcasted_iota(jnp.int32, (LANES, sl), 0) >> (gc.bit_length() - 1)
    lane_g = lax.broadcasted_iota(jnp.int32, (LANES, sl), 1) >> (p.bit_length() - 1)
    diag = row_g == lane_g

    def expand(ref):
        x = ref[...]
        return jnp.where(diag, jnp.concatenate([x] * reps, axis=1), 0.0)

    b_re, b_im = expand(btr_ref), expand(bti_ref)
    c_re, c_im = expand(cr_ref), expand(ci_ref)
    bb_re = coef_re * b_re - coef_im * b_im
    bb_im = coef_re * b_im + coef_im * b_re

    kt = []
    for k in range(CHUNK_T):
        pr, pi = power(float(k))
        ck_re = c_re * pr - c_im * pi
        ck_im = c_re * pi + c_im * pr
        kt.append(_dot_nt(bb_re, ck_re, HIGHEST) - _dot_nt(bb_im, ck_im, HIGHEST))
    zero = jnp.zeros((LANES, LANES), F32)
    for s in range(CHUNK_T):
        pr, pi = power(float(CHUNK_T - 1 - s))
        rows = slice(s * LANES, (s + 1) * LANES)
        ws_ref[rows, :sl] = (pr * bb_re - pi * bb_im).astype(ws_ref.dtype)
        ws_ref[rows, sl:] = (pr * bb_im + pi * bb_re).astype(ws_ref.dtype)
        for t in range(CHUNK_T):
            blk = kt[t - s] if t >= s else zero
            wm_ref[rows, t * LANES:(t + 1) * LANES] = blk.astype(wm_ref.dtype)
    for k in range(CHUNK_T):
        pr, pi = power(float(k + 1))
        ck_re = c_re * pr - c_im * pi
        ck_im = c_re * pi + c_im * pr
        cols = slice(k * LANES, (k + 1) * LANES)
        wc_ref[:sl, cols] = ck_re.T.astype(wc_ref.dtype)
        wc_ref[sl:, cols] = (-ck_im).T.astype(wc_ref.dtype)
    pr, pi = power(float(CHUNK_T))
    at_ref[:, :sl] = pr
    at_ref[:, sl:] = pi


def _s5_prep(log_dt, a_re, a_im, b_re, b_im, c_re, c_im):
    depth, g, p = a_re.shape
    gc = b_re.shape[-1]
    gl = LANES // gc
    ncb = g // gl
    sl = gl * p
    tw = CHUNK_T * LANES
    lane = lambda x: x.reshape(depth, ncb, 1, sl)
    ldt = lane(jnp.repeat(log_dt, p, axis=-1))
    bt = lambda x: jnp.transpose(x, (0, 1, 3, 2)).reshape(depth, ncb, LANES, p)
    cc = lambda x: x.reshape(depth, ncb, LANES, p)
    vec = pl.BlockSpec((None, None, 1, sl), lambda l, c: (l, c, 0, 0))
    mat = pl.BlockSpec((None, None, LANES, p), lambda l, c: (l, c, 0, 0))
    out = lambda r, w: pl.BlockSpec((None, None, r, w), lambda l, c: (l, c, 0, 0))
    return pl.pallas_call(
        functools.partial(_s5_prep_kernel, gc=gc, p=p),
        grid=(depth, ncb),
        in_specs=[vec, vec, vec, mat, mat, mat, mat],
        out_specs=[out(tw, 2 * sl), out(tw, tw), out(2 * sl, tw), out(1, 2 * sl)],
        out_shape=[jax.ShapeDtypeStruct((depth, ncb, tw, 2 * sl), BF16),
                   jax.ShapeDtypeStruct((depth, ncb, tw, tw), BF16),
                   jax.ShapeDtypeStruct((depth, ncb, 2 * sl, tw), BF16),
                   jax.ShapeDtypeStruct((depth, ncb, 1, 2 * sl), F32)],
        compiler_params=_params(("arbitrary", "arbitrary")),
        name="s5_prep",
    )(ldt, lane(a_re), lane(a_im), bt(b_re), bt(b_im), cc(c_re), cc(c_im))


def _cache_cumsum_kernel(lf_ref, c_ref):
    n = lf_ref.shape[0]
    r = lax.broadcasted_iota(jnp.int32, (LANES, LANES), 0)
    c = lax.broadcasted_iota(jnp.int32, (LANES, LANES), 1)
    tri = (c <= r).astype(F32)
    carry = jnp.zeros((1, lf_ref.shape[1]), F32)
    for blk in range(n // LANES):
        rows = slice(blk * LANES, (blk + 1) * LANES)
        cs = jnp.dot(tri, lf_ref[rows, :], precision=HIGHEST, preferred_element_type=F32) + carry
        c_ref[rows, :] = cs
        carry = cs[LANES - 1:LANES, :]


def _cache_cumsum(cache_logf):
    depth, nb, past, nh = cache_logf.shape
    series = depth * nb * nh
    cols = _pick(series, 1024)
    lf_t = jnp.transpose(cache_logf, (2, 0, 1, 3)).reshape(past, series)
    spec = pl.BlockSpec((past, cols), lambda i: (0, i))
    out = pl.pallas_call(
        _cache_cumsum_kernel,
        grid=(series // cols,),
        in_specs=[spec], out_specs=spec,
        out_shape=jax.ShapeDtypeStruct((past, series), F32),
        compiler_params=_params(("arbitrary",)),
        name="cache_cumsum",
    )(lf_t)
    return jnp.transpose(out.reshape(past, depth, nb, nh), (1, 2, 3, 0))


def _inproj_kernel(*refs, tiles_per_seq, stream_rows, nh, hd, n_in):
    x_ref, g_ref, w_ref, wf_ref, bf_ref, qg_ref, kg_ref, cin_ref = refs[:8]
    if stream_rows:
        (u_ref, zs_ref, q_ref, kf_ref, vf_ref, za_ref, lf_ref, c_ref,
         hb_ref, carry_ref, cs_ref) = refs[n_in:]
    else:
        (u_ref, zs_ref, q_ref, kf_ref, vf_ref, kb_ref, vt_ref, ek_ref, za_ref, lf_ref,
         hb_ref, carry_ref, cs_ref) = refs[n_in:]
    i = pl.program_id(0)
    tm = x_ref.shape[0]
    sec = nh * hd

    @pl.when(i % tiles_per_seq == 0)
    def _():
        carry_ref[...] = cin_ref[...]

    hb_ref[...] = _rms(x_ref[...], g_ref[...]).astype(BF16)
    f = _dot(hb_ref[...], wf_ref[...]) + bf_ref[...]
    logf = jnp.minimum(f, 0.0) - jnp.log1p(jnp.exp(-jnp.abs(f)))
    r = lax.broadcasted_iota(jnp.int32, (LANES, LANES), 0)
    c = lax.broadcasted_iota(jnp.int32, (LANES, LANES), 1)
    if stream_rows:
        shift = stream_rows.bit_length() - 1
        same = (r >> shift) == (c >> shift)
        tri = (same & (c <= r)).astype(F32)
        ones = same.astype(F32)
        for blk in range(tm // LANES):
            rows = slice(blk * LANES, (blk + 1) * LANES)
            lb = logf[rows, :]
            cin = carry_ref[rows, :]
            cs_ref[rows, :] = jnp.dot(tri, lb, precision=HIGHEST, preferred_element_type=F32) + cin
            carry_ref[rows, :] = jnp.dot(ones, lb, precision=HIGHEST, preferred_element_type=F32) + cin
    else:
        tri = (c <= r).astype(F32)
        carry = carry_ref[...]
        for blk in range(tm // LANES):
            rows = slice(blk * LANES, (blk + 1) * LANES)
            cs = jnp.dot(tri, logf[rows, :], precision=HIGHEST, preferred_element_type=F32) + carry
            cs_ref[rows, :] = cs
            carry = cs[LANES - 1:LANES, :]
        carry_ref[...] = carry
    lf_ref[...] = logf[:, :nh].reshape(lf_ref.shape)
    if stream_rows:
        c_ref[...] = cs_ref[:, :nh]

    def section(s):
        return _dot(hb_ref[...], w_ref[:, s * sec:(s + 1) * sec])

    def heads_norm(res, g):
        return jnp.concatenate(
            [_rms(res[:, h * hd:(h + 1) * hd], g) for h in range(nh)], axis=1)

    def to_final(ref, val):
        for h in range(nh):
            blk = val[:, h * hd:(h + 1) * hd]
            if stream_rows:
                for b in range(tm // stream_rows):
                    ref[b, pl.ds(h, stream_rows, stride=nh), :] = blk[b * stream_rows:(b + 1) * stream_rows]
            else:
                ref[pl.ds(h, tm, stride=nh), :] = blk

    def bias_block(parts, first):
        r = lax.broadcasted_iota(jnp.int32, (LANES, LANES), 0)
        c = lax.broadcasted_iota(jnp.int32, (LANES, LANES), 1)
        out = None
        for k, part in enumerate(parts):
            sel = ((c == BIAS_LANES * r + first + k) & (r < nh)).astype(BF16)
            term = _dot(part, sel)
            out = term if out is None else out + term
        return out

    lane = lax.broadcasted_iota(jnp.int32, (1, LANES), 1)
    lane_head = lane >> 3
    lane_slot = lane & (BIAS_LANES - 1)

    res = section(0)
    for cb in range(sec // LANES):
        u_ref[cb] = res[:, cb * LANES:(cb + 1) * LANES]
    zs_ref[...] = section(1)

    if stream_rows:
        q_ref[...] = heads_norm(section(2), qg_ref[...]) * (hd ** -0.5)
    else:
        parts = _split3(cs_ref[...] * LOG2E)
        qn = heads_norm(section(2), qg_ref[...]) * (hd ** -0.5 * LOG2E)
        ones = jnp.where((lane_slot >= 3) & (lane_slot < 6) & (lane_head < nh), 1.0, 0.0)
        eq = bias_block(parts, 0) + ones
        for h in range(nh):
            q_ref[:, 2 * h * hd:(2 * h + 1) * hd] = qn[:, h * hd:(h + 1) * hd].astype(BF16)
            q_ref[:, (2 * h + 1) * hd:(2 * h + 2) * hd] = jnp.where(lane_head == h, eq, 0.0).astype(BF16)

    kn = heads_norm(section(3), kg_ref[...])
    to_final(kf_ref, kn)
    if not stream_rows:
        kb_ref[...] = kn.astype(BF16)
        ones = jnp.where((lane_slot < 3) & (lane_head < nh), 1.0, 0.0)
        ek_ref[...] = (ones - bias_block(parts, 3)).astype(BF16)

    res = section(4)
    to_final(vf_ref, res)
    if not stream_rows:
        vt_ref[...] = res.T.astype(BF16)
    za_ref[...] = section(5)


def _inproj(x, g, w_main, w_f, b_f, qg, kg, c_init, k_stack, v_stack, layer, depth, *,
            tm, tiles_per_seq, stream_rows, nh):
    n, d = x.shape
    hd = qg.shape[-1]
    sec = nh * hd
    assert hd == LANES and nh * BIAS_LANES <= LANES and tm % LANES == 0 and w_main.shape[-1] >= 6 * sec
    nt = n // tm
    ncb = sec // LANES
    row = lambda w: pl.BlockSpec((tm, w), lambda i: (i, 0))
    const = lambda a: pl.BlockSpec(a.shape, lambda i: (0,) * a.ndim)
    shape = jax.ShapeDtypeStruct
    u_spec = pl.BlockSpec((ncb, tm, LANES), lambda i: (0, i, 0))
    if stream_rows:
        nb = tm // stream_rows
        fin_shape = (depth, nb, nt, stream_rows * nh, hd)
        fin_spec = pl.BlockSpec((None, nb, None, stream_rows * nh, hd), lambda i: (layer, 0, i, 0, 0))
        out_specs = [u_spec, row(sec), row(sec), fin_spec, fin_spec, row(sec),
                     pl.BlockSpec((nb, None, stream_rows, nh), lambda i: (0, i, 0, 0)), row(nh)]
        out_shape = [shape((ncb, n, LANES), F32), shape((n, sec), F32), shape((n, sec), F32),
                     shape(fin_shape, F32), shape(fin_shape, F32), shape((n, sec), F32),
                     shape((nb, nt, stream_rows, nh), F32), shape((n, nh), F32)]
    else:
        fin_shape = (depth, n * nh, hd)
        fin_spec = pl.BlockSpec((None, tm * nh, hd), lambda i: (layer, i, 0))
        out_specs = [u_spec, row(sec), row(2 * sec), fin_spec, fin_spec, row(sec),
                     pl.BlockSpec((sec, tm), lambda i: (0, i)), row(LANES), row(sec), row(nh)]
        out_shape = [shape((ncb, n, LANES), F32), shape((n, sec), F32), shape((n, 2 * sec), BF16),
                     shape(fin_shape, F32), shape(fin_shape, F32), shape((n, sec), BF16),
                     shape((sec, n), BF16), shape((n, LANES), BF16), shape((n, sec), F32),
                     shape((n, nh), F32)]
    args = [x, g, w_main, w_f, b_f, qg, kg, c_init]
    lspec = lambda a: _layer_spec(a, layer)
    in_specs = [row(d), lspec(g), _layer_spec(w_main, layer, pipeline_mode=pl.Buffered(1)),
                lspec(w_f), lspec(b_f), lspec(qg), lspec(kg), const(c_init)]
    aliases = {}
    if k_stack is not None:
        args += [k_stack, v_stack]
        in_specs += [pl.BlockSpec(memory_space=pl.ANY)] * 2
        aliases = {8: 3, 9: 4}
    return pl.pallas_call(
        functools.partial(_inproj_kernel, tiles_per_seq=tiles_per_seq, stream_rows=stream_rows,
                          nh=nh, hd=hd, n_in=len(args)),
        grid=(nt,),
        in_specs=in_specs, out_specs=out_specs, out_shape=out_shape,
        input_output_aliases=aliases,
        scratch_shapes=[pltpu.VMEM((tm, d), BF16), pltpu.VMEM(c_init.shape, F32),
                        pltpu.VMEM((tm, LANES), F32)],
        compiler_params=_params(("arbitrary",)),
        name="inproj",
    )(*args)


def _ssm_kernel(u_ref, ws_ref, wm_ref, wc_ref, at_ref, h0_ref, y_ref, ht_ref,
                uc_ref, s_ref, hp_ref, h_ref, *, nb):
    t = pl.program_id(2)
    r = uc_ref.shape[0]
    sl = at_ref.shape[-1] // 2
    groups = r // SUBLANES
    span = SUBLANES * CHUNK_T

    def gather(g, _):
        rows = pl.ds(pl.multiple_of(g * SUBLANES, SUBLANES), SUBLANES)
        for s in range(CHUNK_T):
            uc_ref[rows, s * LANES:(s + 1) * LANES] = u_ref[pl.ds(g * span + s, SUBLANES, stride=CHUNK_T), :]
        return 0
    lax.fori_loop(0, groups, gather, 0, unroll=LOOP_UNROLL)

    @pl.when(t == 0)
    def _():
        h_ref[...] = h0_ref[...]

    ub = uc_ref[...].astype(BF16)
    s_ref[...] = _dot(ub, ws_ref[...])
    are = at_ref[:, :sl]
    aim = at_ref[:, sl:]

    def step(hre, him, sre, sim):
        return are * hre - aim * him + sre, are * him + aim * hre + sim

    if nb % SUBLANES == 0:
        def body(j, carry):
            hre, him = carry
            rows = pl.ds(pl.multiple_of(j * nb, nb), nb)
            hp_ref[rows, :sl] = hre
            hp_ref[rows, sl:] = him
            return step(hre, him, s_ref[rows, :sl], s_ref[rows, sl:])
        hre, him = lax.fori_loop(0, r // nb, body, (h_ref[:, :sl], h_ref[:, sl:]))
    else:
        assert nb == 1

        def body(j, carry):
            hre, him = carry
            rows = pl.ds(pl.multiple_of(j * SUBLANES, SUBLANES), SUBLANES)
            st = s_ref[rows, :]
            pre, pim = [], []
            for q in range(SUBLANES):
                pre.append(hre)
                pim.append(him)
                hre, him = step(hre, him, st[q:q + 1, :sl], st[q:q + 1, sl:])
            hp_ref[rows, :sl] = jnp.concatenate(pre, axis=0)
            hp_ref[rows, sl:] = jnp.concatenate(pim, axis=0)
            return hre, him
        hre, him = lax.fori_loop(0, groups, body, (h_ref[:, :sl], h_ref[:, sl:]))
    h_ref[:, :sl] = hre
    h_ref[:, sl:] = him
    uc_ref[...] = _dot(ub, wm_ref[...]) + _dot(hp_ref[...].astype(BF16), wc_ref[...])

    def scatter(g, _):
        rows = pl.ds(pl.multiple_of(g * SUBLANES, SUBLANES), SUBLANES)
        for s in range(CHUNK_T):
            y_ref[pl.ds(g * span + s, SUBLANES, stride=CHUNK_T), :] = uc_ref[rows, s * LANES:(s + 1) * LANES]
        return 0
    lax.fori_loop(0, groups, scatter, 0, unroll=LOOP_UNROLL)

    @pl.when(t == pl.num_programs(2) - 1)
    def _():
        ht_ref[...] = h_ref[...]


def _ssm(u, ws, wm, wc, at, h0, layer, *, nseq, nb, rt):
    ncb, n, _ = u.shape
    tw = CHUNK_T * LANES
    nt = n // (CHUNK_T * nseq * rt)
    sl2 = at.shape[-1]
    wspec = lambda a: pl.BlockSpec((None, None) + a.shape[2:], lambda c, s, t: (layer, c, 0, 0))
    rows = pl.BlockSpec((None, rt * CHUNK_T, LANES), lambda c, s, t: (c, s * nt + t, 0))
    hspec = pl.BlockSpec((None, None, nb, sl2), lambda c, s, t: (c, s, 0, 0))
    return pl.pallas_call(
        functools.partial(_ssm_kernel, nb=nb),
        grid=(ncb, nseq, nt),
        in_specs=[rows, wspec(ws), wspec(wm), wspec(wc), wspec(at), hspec],
        out_specs=[rows, hspec],
        out_shape=[jax.ShapeDtypeStruct((ncb, n, LANES), F32),
                   jax.ShapeDtypeStruct((ncb, nseq, nb, sl2), F32)],
        scratch_shapes=[pltpu.VMEM((rt, tw), F32), pltpu.VMEM((rt, sl2), F32),
                        pltpu.VMEM((rt, sl2), F32), pltpu.VMEM((nb, sl2), F32)],
        compiler_params=_params(("arbitrary", "arbitrary", "arbitrary")),
        name="ssm",
    )(u, ws, wm, wc, at, h0)


def _attn_prompt_kernel(qi_ref, ki_ref, kind_ref, q_ref, k_ref, ek_ref, vt_ref, za_ref, ng_ref, o_ref,
                        m_ref, acc_ref, *, nh, hd):
    qi = qi_ref[pl.program_id(1)]
    ki = ki_ref[pl.program_id(1)]
    kind = kind_ref[pl.program_id(1)]
    tq, tk = q_ref.shape[0], k_ref.shape[0]

    @pl.when(ki == 0)
    def _():
        m_ref[...] = jnp.full(m_ref.shape, NEG_BIG, F32)
        acc_ref[...] = jnp.zeros(acc_ref.shape, F32)

    def tile(diagonal):
        ek = ek_ref[...]
        ones = jnp.ones((acc_ref.shape[1] - hd, tk), BF16)
        if diagonal:
            visible = (lax.broadcasted_iota(jnp.int32, (tk, tq), 0) + ki * tk
                       <= lax.broadcasted_iota(jnp.int32, (tk, tq), 1) + qi * tq)

        def scores(h):
            ka = jnp.concatenate([k_ref[:, h * hd:(h + 1) * hd], ek], axis=1)
            return _dot_nt(ka, q_ref[:, 2 * h * hd:2 * (h + 1) * hd])

        s_next = scores(0)
        for h in range(nh):
            s = s_next
            if h + 1 < nh:
                s_next = scores(h + 1)
            if diagonal:
                s = jnp.where(visible, s, NEG_BIG)
            m_old = m_ref[h]
            m_new = jnp.maximum(m_old, jnp.max(s, axis=0, keepdims=True))
            p = jnp.exp2(s - m_new).astype(BF16)
            va = jnp.concatenate([vt_ref[h * hd:(h + 1) * hd, :], ones], axis=0)
            acc_ref[h] = jnp.exp2(m_old - m_new) * acc_ref[h] + _dot(va, p)
            m_ref[h] = m_new

    @pl.when(kind == 0)
    def _():
        tile(False)

    @pl.when(kind != 0)
    def _():
        tile(True)

    @pl.when(kind == 2)
    def _():
        o = jnp.concatenate([(acc_ref[h, :hd, :] / acc_ref[h, hd:hd + 1, :]).T for h in range(nh)], axis=1)
        o_ref[...] = _rms(o * _silu(za_ref[...]), ng_ref[...])


def _attn_prompt(qa, kb, ek, vt, za, ng, layer, *, nseq, nh, tq, tk):
    n, da = kb.shape
    l = n // nseq
    nq, nk = l // tq, l // tk
    hd = da // nh
    sum_rows = 2 * SUBLANES
    steps = []
    for i in range(nq):
        last = ((i + 1) * tq - 1) // tk
        for j in range(last + 1):
            steps.append((i, j, 2 if j == last else int((j + 1) * tk - 1 > i * tq)))
    tabs = [jnp.asarray([s[c] for s in steps], jnp.int32) for c in range(3)]
    qrow = lambda w: pl.BlockSpec((tq, w), lambda b, s, qt, kt, kd: (b * nq + qt[s], 0))
    krow = lambda w: pl.BlockSpec((tk, w), lambda b, s, qt, kt, kd: (b * nk + kt[s], 0))
    return pl.pallas_call(
        functools.partial(_attn_prompt_kernel, nh=nh, hd=hd),
        grid_spec=pltpu.PrefetchScalarGridSpec(
            num_scalar_prefetch=3,
            grid=(nseq, len(steps)),
            in_specs=[qrow(2 * da), krow(da), krow(LANES),
                      pl.BlockSpec((da, tk), lambda b, s, qt, kt, kd: (0, b * nk + kt[s])),
                      qrow(da), _layer_spec(ng, layer)],
            out_specs=qrow(da),
            scratch_shapes=[pltpu.VMEM((nh, 1, tq), F32), pltpu.VMEM((nh, hd + sum_rows, tq), F32)]),
        out_shape=jax.ShapeDtypeStruct((n, da), F32),
        compiler_params=_params(("arbitrary", "arbitrary")),
        name="attn_prompt",
    )(*tabs, qa, kb, ek, vt, za, ng)


def _attn_sample_kernel(q_ref, kp_ref, vp_ref, kn_ref, vn_ref, cq_ref, ckp_ref, ckn_ref,
                        za_ref, ng_ref, o_ref, *, nh, hd):
    nj, ns, da = q_ref.shape
    tq = nj * ns
    past = kp_ref.shape[0] // nh
    q = q_ref[...].reshape(tq, da).astype(BF16)
    cq = cq_ref[...].reshape(tq, nh)
    za = za_ref[...].reshape(tq, da)
    visible = (lax.broadcasted_iota(jnp.int32, (tq, tq), 1)
               <= lax.broadcasted_iota(jnp.int32, (tq, tq), 0))
    outs = []
    for h in range(nh):
        head = lambda ref, rows: ref[pl.ds(h, rows, stride=nh), :].astype(BF16)
        qh = q[:, h * hd:(h + 1) * hd]
        cqh = cq[:, h:h + 1]
        sp = _dot_nt(qh, head(kp_ref, past)) + (cqh - ckp_ref[h:h + 1, :])
        sn = _dot_nt(qh, head(kn_ref, tq)) + (cqh - ckn_ref[h:h + 1, :])
        sn = jnp.where(visible, sn, NEG_BIG)
        m = jnp.maximum(jnp.max(sp, axis=-1, keepdims=True), jnp.max(sn, axis=-1, keepdims=True))
        pp = jnp.exp(sp - m)
        pn = jnp.exp(sn - m)
        l = jnp.sum(pp, axis=-1, keepdims=True) + jnp.sum(pn, axis=-1, keepdims=True)
        o = _dot(pp.astype(BF16), head(vp_ref, past)) + _dot(pn.astype(BF16), head(vn_ref, tq))
        outs.append(o / l)
    o = jnp.concatenate(outs, axis=1)
    o_ref[...] = _rms(o * _silu(za), ng_ref[...]).reshape(o_ref.shape)


def _attn_sample(q, k_past, v_past, k_stack, v_stack, cq, ck_past_t, ck_new_t, za, ng, layer, *, nb, nh):
    n, da = q.shape
    hd = da // nh
    depth, _, nj, rows, _ = k_stack.shape
    ns = rows // nh
    past = ck_past_t.shape[-1]
    view = lambda a: a.reshape(nj, nb, ns, a.shape[-1])
    new = lambda a: a.reshape(depth, nb, nj * rows, hd)
    qspec = lambda w: pl.BlockSpec((nj, None, ns, w), lambda b: (0, b, 0, 0))
    pspec = pl.BlockSpec((None, None, past * nh, hd), lambda b: (layer, b, 0, 0))
    nspec = pl.BlockSpec((None, None, nj * rows, hd), lambda b: (layer, b, 0, 0))
    out = pl.pallas_call(
        functools.partial(_attn_sample_kernel, nh=nh, hd=hd),
        grid=(nb,),
        in_specs=[qspec(da), pspec, pspec, nspec, nspec, qspec(nh),
                  pl.BlockSpec((None, None, nh, past), lambda b: (layer, b, 0, 0)),
                  pl.BlockSpec((None, nh, nj * ns), lambda b: (b, 0, 0)),
                  qspec(da), _layer_spec(ng, layer)],
        out_specs=qspec(da),
        out_shape=jax.ShapeDtypeStruct((nj, nb, ns, da), F32),
        compiler_params=_params(("arbitrary",)),
        name="attn_sample",
    )(view(q), k_past, v_past, new(k_stack), new(v_stack), view(cq), ck_past_t, ck_new_t, view(za), ng)
    return out.reshape(n, da)


def _out_kernel(x_ref, y_ref, u_ref, zs_ref, ya_ref, d_ref, wg_ref, bg_ref, sng_ref, wo_ref, o_ref):
    ds = zs_ref.shape[1]
    ys = jnp.concatenate([y_ref[cb] + d_ref[:, cb * LANES:(cb + 1) * LANES] * u_ref[cb]
                          for cb in range(y_ref.shape[0])], axis=1)
    zs = 0.5 * ys * (1.0 + lax.erf(ys * math.sqrt(0.5)))
    gate = jax.nn.sigmoid(_dot(zs.astype(BF16), wg_ref[...]) + bg_ref[...])
    y_ssm = _rms(zs * gate * _silu(zs_ref[...]), sng_ref[...])
    o_ref[...] = (x_ref[...] + _dot(y_ssm.astype(BF16), wo_ref[:ds, :])
                  + _dot(ya_ref[...].astype(BF16), wo_ref[ds:, :]))


def _out_stage(x, y, u, zs, ya, d, w_glu, b_glu, sng, w_out, layer, *, tm):
    n, dm = x.shape
    ds = zs.shape[1]
    row = lambda w: pl.BlockSpec((tm, w), lambda i: (i, 0))
    blk3 = pl.BlockSpec((y.shape[0], tm, LANES), lambda i: (0, i, 0))
    lspec = lambda a: _layer_spec(a, layer)
    resident = lambda a: _layer_spec(a, layer, pipeline_mode=pl.Buffered(1))
    return pl.pallas_call(
        _out_kernel,
        grid=(n // tm,),
        in_specs=[row(dm), blk3, blk3, row(ds), row(ya.shape[1]), lspec(d), resident(w_glu),
                  lspec(b_glu), lspec(sng), resident(w_out)],
        out_specs=row(dm),
        out_shape=jax.ShapeDtypeStruct((n, dm), F32),
        compiler_params=_params(("arbitrary",)),
        name="out_stage",
    )(x, y, u, zs, ya, d, w_glu, b_glu, sng, w_out)


def _pick(n, pref):
    t = min(n, pref)
    while n % t:
        t //= 2
    return t


def kernel(x_prompt, x_sample, cache_k, cache_v, cache_logf, state_ssm_re, state_ssm_im, norm_gain, w_in, b_f, q_norm_gain, k_norm_gain, ssm_log_dt, ssm_a_re, ssm_a_im, ssm_b_re, ssm_b_im, ssm_c_re, ssm_c_im, ssm_d, w_glu, b_glu, ssm_out_norm, att_out_norm, w_out):
    bsz, seq, dm = x_prompt.shape
    nb, dseq, _ = x_sample.shape
    depth, _, past, nh, hd = cache_k.shape
    _, g, p = ssm_a_re.shape
    d_ssm = w_glu.shape[-1]
    d_att = nh * hd
    assert d_ssm == d_att and w_in.shape[-1] == 2 * d_ssm + 4 * d_att + nh
    assert dseq % CHUNK_T == 0 and seq % CHUNK_T == 0 and nb % SUBLANES == 0
    ncb = d_ssm // LANES
    sl2 = 2 * (LANES // ssm_b_re.shape[-1]) * p
    nj = dseq // CHUNK_T
    n_p, n_s = bsz * seq, nb * dseq

    w_main = w_in.astype(BF16)
    w_f = jnp.pad(w_in[:, :, 6 * d_ssm:], ((0, 0), (0, 0), (0, LANES - nh))).astype(BF16)
    b_fp = jnp.pad(b_f, ((0, 0), (0, LANES - nh)))[:, None, :]
    vec = lambda a: a.reshape(depth, 1, -1)
    common = (vec(norm_gain), w_main, w_f, b_fp, vec(q_norm_gain), vec(k_norm_gain))
    tail = (vec(ssm_d), w_glu.astype(BF16), vec(b_glu), vec(ssm_out_norm), w_out.astype(BF16))
    att_ng = vec(att_out_norm)
    ws, wm, wc, at = _s5_prep(ssm_log_dt, ssm_a_re, ssm_a_im, ssm_b_re, ssm_b_im, ssm_c_re, ssm_c_im)

    ck_past_t = _cache_cumsum(cache_logf)
    c_end = jnp.pad(ck_past_t[..., -1], ((0, 0), (0, 0), (0, LANES - nh)))
    c_end = jnp.repeat(c_end, CHUNK_T, axis=1)
    k_past = cache_k.reshape(depth, nb, past * nh, hd)
    v_past = cache_v.reshape(depth, nb, past * nh, hd)
    to_lanes = lambda a: a.reshape(nb, ncb, sl2 // 2)
    zero_row = jnp.zeros((1, LANES), F32)
    h0_p = jnp.zeros((ncb, bsz, 1, sl2), F32)

    tm_p = _pick(seq, 256)
    tm_out = _pick(seq, 512)
    tq = _pick(seq, 512)
    tk = _pick(seq, 1024)
    rt_p = _pick(seq // CHUNK_T, 1024)
    tm_s = nb * CHUNK_T

    xp = x_prompt.reshape(n_p, dm)
    xs = jnp.transpose(x_sample.reshape(nb, nj, CHUNK_T, dm), (1, 0, 2, 3)).reshape(n_s, dm)
    kp_stack = vp_stack = ks_stack = vs_stack = None
    outs = [[] for _ in range(6)]
    for l in range(depth):
        u, zs, qa, kp_stack, vp_stack, kb, vt, ek, za, lf = _inproj(
            xp, *common, zero_row, kp_stack, vp_stack, l, depth,
            tm=tm_p, tiles_per_seq=seq // tm_p, stream_rows=0, nh=nh)
        y, ht = _ssm(u, ws, wm, wc, at, h0_p, l, nseq=bsz, nb=1, rt=rt_p)
        ya = _attn_prompt(qa, kb, ek, vt, za, att_ng, l, nseq=bsz, nh=nh, tq=tq, tk=tk)
        xp = _out_stage(xp, y, u, zs, ya, *tail, l, tm=tm_out)
        ht = jnp.transpose(ht[:, :, 0], (1, 0, 2)).reshape(bsz, ncb, 2, g // ncb, p)
        for lst, val in zip(outs[:3], (lf, ht[:, :, 0].reshape(bsz, g, p), ht[:, :, 1].reshape(bsz, g, p))):
            lst.append(val)

        h0_s = jnp.transpose(jnp.concatenate([to_lanes(state_ssm_re[l]), to_lanes(state_ssm_im[l])], axis=-1),
                             (1, 0, 2))[:, None]
        u, zs, q, ks_stack, vs_stack, za, lf, cs = _inproj(
            xs, *common, c_end[l], ks_stack, vs_stack, l, depth,
            tm=tm_s, tiles_per_seq=nj, stream_rows=CHUNK_T, nh=nh)
        y, ht = _ssm(u, ws, wm, wc, at, h0_s, l, nseq=1, nb=nb, rt=nj * nb)
        ck_new_t = jnp.transpose(cs.reshape(nj, nb, CHUNK_T, nh), (1, 3, 0, 2)).reshape(nb, nh, dseq)
        ya = _attn_sample(q, k_past, v_past, ks_stack, vs_stack, cs, ck_past_t, ck_new_t, za,
                          att_ng, l, nb=nb, nh=nh)
        xs = _out_stage(xs, y, u, zs, ya, *tail, l, tm=tm_s)
        ht = jnp.transpose(ht[:, 0], (1, 0, 2)).reshape(nb, ncb, 2, g // ncb, p)
        for lst, val in zip(outs[3:], (lf, ht[:, :, 0].reshape(nb, g, p), ht[:, :, 1].reshape(nb, g, p))):
            lst.append(val)

    y_prompt = xp.reshape(bsz, seq, dm)
    y_sample = jnp.transpose(xs.reshape(nj, nb, CHUNK_T, dm), (1, 0, 2, 3)).reshape(nb, dseq, dm)
    st = lambda lst, shape: jnp.stack(lst).reshape((depth,) + shape)
    return (y_prompt, y_sample,
            kp_stack.reshape(depth, bsz, seq, nh, hd), vp_stack.reshape(depth, bsz, seq, nh, hd),
            st(outs[0], (bsz, seq, nh)), st(outs[1], (bsz, g, p)), st(outs[2], (bsz, g, p)),
            ks_stack.reshape(depth, nb, dseq, nh, hd), vs_stack.reshape(depth, nb, dseq, nh, hd),
            st(outs[3], (nb, dseq, nh)), st(outs[4], (nb, g, p)), st(outs[5], (nb, g, p)))
```
